```python
import jax, jax.numpy as jnp
from jax import lax
import numpy as np

D_MODEL = 2048
BATCH = 2
SEQ = 4096
DEPTH = 2

CONV_CH = 1024
CONV_GROUPS = 8
CONV_K = 31
MLA_HEADS = 8
QK_NOPE = 128
QK_ROPE = 64
V_HEAD = 128
Q_RANK = 512
KV_RANK = 512
ROPE_BASE = 10000.0
Q_BLOCK = 128
MLA_WIDTH = MLA_HEADS * V_HEAD
HGRN_HEADS = 16
HGRN_EXPAND = 128
HGRN_FDIM = HGRN_HEADS * HGRN_EXPAND
HGRN_HEAD_V = D_MODEL // HGRN_HEADS
HGRN_CHUNK = 64
EPS = 1e-6

N_EVEN = (DEPTH + 1) // 2
N_ODD = DEPTH // 2
EVEN_MIX = CONV_CH + MLA_WIDTH
EVEN_IN = 3 * CONV_CH + Q_RANK + KV_RANK + QK_ROPE + MLA_WIDTH
ODD_IN = 2 * HGRN_FDIM + 2 * D_MODEL

kernel_name = "hybrid_conv_mla_hgrn2_gated"


def _split(p, sizes):
    out, off = [], 0
    for s in sizes:
        out.append(p[..., off:off + s])
        off += s
    return out


def rmsnorm(x, g):
    xf = x.astype(jnp.float32)
    y = xf * lax.rsqrt(jnp.mean(xf * xf, axis=-1, keepdims=True) + EPS)
    return (y * g.astype(jnp.float32)).astype(x.dtype)


def rope_tables(seq):
    inv_freq = 1.0 / (ROPE_BASE ** (jnp.arange(0, QK_ROPE, 2, dtype=jnp.float32) / QK_ROPE))
    ang = jnp.arange(seq, dtype=jnp.float32)[:, None] * inv_freq[None, :]
    return jnp.cos(ang), jnp.sin(ang)


def apply_rope(x, cos, sin):
    x1, x2 = jnp.split(x.astype(jnp.float32), 2, axis=-1)
    c = cos[None, :, None, :]
    s = sin[None, :, None, :]
    return jnp.concatenate([x1 * c - x2 * s, x1 * s + x2 * c], axis=-1).astype(x.dtype)


def conformer_conv(v, g_glu, z, w_dw, b_dw, ln_g, ln_b):
    u = v * jax.nn.sigmoid(g_glu)
    u = jnp.pad(u, ((0, 0), (CONV_K - 1, 0), (0, 0)))
    y = lax.conv_general_dilated(u, w_dw[:, None, :].astype(u.dtype), window_strides=(1,),
                                 padding='VALID', dimension_numbers=('NWC', 'WIO', 'NWC'),
                                 feature_group_count=CONV_CH) + b_dw
    B, S, C = y.shape
    yf = y.astype(jnp.float32).reshape(B, S, CONV_GROUPS, C // CONV_GROUPS)
    mu = jnp.mean(yf, axis=-1, keepdims=True)
    var = jnp.mean(jnp.square(yf - mu), axis=-1, keepdims=True)
    yf = ((yf - mu) * lax.rsqrt(var + EPS)).reshape(B, S, C)
    y = (yf * ln_g.astype(jnp.float32) + ln_b.astype(jnp.float32)).astype(v.dtype)
    return jax.nn.silu(y) * jax.nn.silu(z)


def mla(c_q, c_kv, k_pe, q_norm_g, w_uq, kv_norm_g, w_ukv, cos, sin):
    B, S, _ = c_q.shape
    q = jnp.einsum('bsr,rhd->bshd', rmsnorm(c_q, q_norm_g), w_uq)
    q_nope, q_pe = q[..., :QK_NOPE], apply_rope(q[..., QK_NOPE:], cos, sin)
    kv = jnp.einsum('bsr,rhd->bshd', rmsnorm(c_kv, kv_norm_g), w_ukv)
    k_nope, v = kv[..., :QK_NOPE], kv[..., QK_NOPE:]
    k_pe = apply_rope(k_pe[:, :, None, :], cos, sin)[:, :, 0, :]
    scale = (QK_NOPE + QK_ROPE) ** -0.5
    n_blk = S // Q_BLOCK
    qn_b = q_nope.reshape(B, n_blk, Q_BLOCK, MLA_HEADS, QK_NOPE).transpose(1, 0, 2, 3, 4)
    qp_b = q_pe.reshape(B, n_blk, Q_BLOCK, MLA_HEADS, QK_ROPE).transpose(1, 0, 2, 3, 4)
    kpos = jnp.arange(S)

    def block(args):
        qn, qp, i = args
        s = (jnp.einsum('bqhd,bkhd->bhqk', qn, k_nope) +
             jnp.einsum('bqhr,bkr->bhqk', qp, k_pe)).astype(jnp.float32) * scale
        qpos = i * Q_BLOCK + jnp.arange(Q_BLOCK)
        mask = kpos[None, :] <= qpos[:, None]
        s = jnp.where(mask[None, None], s, -jnp.inf)
        p = jax.nn.softmax(s, axis=-1).astype(v.dtype)
        return jnp.einsum('bhqk,bkhd->bqhd', p, v)

    o = lax.map(block, (qn_b, qp_b, jnp.arange(n_blk)))
    return o.transpose(1, 0, 2, 3, 4).reshape(B, S, MLA_WIDTH)


def hgrn2(q, f_raw, i_val, lb, norm_g):
    B, S, _ = q.shape
    H, K, V, C = HGRN_HEADS, HGRN_EXPAND, HGRN_HEAD_V, HGRN_CHUNK
    nC = S // C
    logf = jnp.logaddexp(jnp.log(lb), jnp.log1p(-lb) + jax.nn.log_sigmoid(f_raw.astype(jnp.float32)))
    k = -jnp.expm1(logf)

    def chunks(t, d):
        return t.astype(jnp.float32).reshape(B, nC, C, H, d).transpose(1, 0, 3, 2, 4)

    qc, kc, gc, vc = chunks(q, K), chunks(k, K), chunks(logf, K), chunks(i_val, V)
    tri = jnp.tril(jnp.ones((C, C), dtype=bool))

    def step(state, inp):
        qq, kk, gg, vv = inp
        b = jnp.cumsum(gg, axis=2)
        diff = b[:, :, :, None, :] - b[:, :, None, :, :]
        decay = jnp.exp(jnp.where(tri[None, None, :, :, None], diff, -jnp.inf))
        A = jnp.einsum('bhtk,bhsk,bhtsk->bhts', qq, kk, decay)
        o = (jnp.einsum('bhts,bhsv->bhtv', A, vv) +
             jnp.einsum('bhtk,bhkv->bhtv', qq * jnp.exp(b), state))
        b_last = b[:, :, -1:, :]
        new_state = (jnp.exp(b_last)[:, :, 0, :, None] * state +
                     jnp.einsum('bhsk,bhsv->bhkv', kk * jnp.exp(b_last - b), vv))
        return new_state, o

    state0 = jnp.zeros((B, H, K, V), jnp.float32)
    _, o = lax.scan(step, state0, (qc, kc, gc, vc))
    o = o.transpose(1, 0, 3, 2, 4).reshape(B, S, H, V).astype(q.dtype)
    return rmsnorm(o, norm_g).reshape(B, S, H * V)


def setup_inputs(seed: int = 0) -> dict:
    key = jax.random.key(seed)
    ks = jax.random.split(key, 18)

    def nrm(k, shape, scale):
        return jax.random.normal(k, shape, jnp.float32) * scale

    def gain(k, shape):
        return 1.0 + 0.02 * jax.random.normal(k, shape, jnp.float32)

    return {
        "x": nrm(ks[0], (BATCH, SEQ, D_MODEL), 1.0),
        "ev_norm_g": gain(ks[1], (N_EVEN, D_MODEL)),
        "ev_w_in": nrm(ks[2], (N_EVEN, D_MODEL, EVEN_IN), D_MODEL ** -0.5),
        "conv_w": nrm(ks[3], (N_EVEN, CONV_K, CONV_CH), CONV_K ** -0.5),
        "conv_b": nrm(ks[4], (N_EVEN, CONV_CH), 0.02),
        "conv_ln_g": gain(ks[5], (N_EVEN, CONV_CH)),
        "conv_ln_b": nrm(ks[6], (N_EVEN, CONV_CH), 0.02),
        "mla_q_norm_g": gain(ks[7], (N_EVEN, Q_RANK)),
        "mla_w_uq": nrm(ks[8], (N_EVEN, Q_RANK, MLA_HEADS, QK_NOPE + QK_ROPE), Q_RANK ** -0.5),
        "mla_kv_norm_g": gain(ks[9], (N_EVEN, KV_RANK)),
        "mla_w_ukv": nrm(ks[10], (N_EVEN, KV_RANK, MLA_HEADS, QK_NOPE + V_HEAD), KV_RANK ** -0.5),
        "ev_w_out": nrm(ks[11], (N_EVEN, EVEN_MIX, D_MODEL), EVEN_MIX ** -0.5),
        "od_norm_g": gain(ks[12], (N_ODD, D_MODEL)),
        "od_w_in": nrm(ks[13], (N_ODD, D_MODEL, ODD_IN), D_MODEL ** -0.5),
        "hgrn_lb_logits": nrm(ks[14], (DEPTH, HGRN_FDIM), 0.5),
        "hgrn_norm_g": gain(ks[15], (N_ODD, HGRN_HEAD_V)),
        "od_w_out": nrm(ks[16], (N_ODD, D_MODEL, D_MODEL), D_MODEL ** -0.5),
        "final_norm_g": gain(ks[17], (D_MODEL,)),
    }


def reference(x, ev_norm_g, ev_w_in, conv_w, conv_b, conv_ln_g, conv_ln_b, mla_q_norm_g, mla_w_uq,
              mla_kv_norm_g, mla_w_ukv, ev_w_out, od_norm_g, od_w_in, hgrn_lb_logits, hgrn_norm_g,
              od_w_out, final_norm_g):
    S = x.shape[1]
    cos, sin = rope_tables(S)
    lb_all = jnp.cumsum(jax.nn.softmax(hgrn_lb_logits.astype(jnp.float32), axis=0), axis=0)
    lb_all = lb_all - lb_all[0:1]
    for l in range(DEPTH):
        if l % 2 == 0:
            j = l // 2
            h = rmsnorm(x, ev_norm_g[j])
            p = h @ ev_w_in[j]
            a_v, a_g, a_z, c_q, c_kv, k_pe, b_z = _split(
                p, [CONV_CH, CONV_CH, CONV_CH, Q_RANK, KV_RANK, QK_ROPE, MLA_WIDTH])
            a_out = conformer_conv(a_v, a_g, a_z, conv_w[j], conv_b[j], conv_ln_g[j], conv_ln_b[j])
            b_out = mla(c_q, c_kv, k_pe, mla_q_norm_g[j], mla_w_uq[j], mla_kv_norm_g[j],
                        mla_w_ukv[j], cos, sin) * jax.nn.silu(b_z)
            x = x + jnp.concatenate([a_out, b_out], axis=-1) @ ev_w_out[j]
        else:
            j = l // 2
            h = rmsnorm(x, od_norm_g[j])
            p = h @ od_w_in[j]
            q, f_raw, i_val, g = _split(p, [HGRN_FDIM, HGRN_FDIM, D_MODEL, D_MODEL])
            o = hgrn2(q, f_raw, i_val, lb_all[l], hgrn_norm_g[j]) * jax.nn.silu(g)
            x = x + o @ od_w_out[j]
    return rmsnorm(x, final_norm_g)
```

```python
import functools
import math

import numpy as np
import jax
import jax.numpy as jnp
from jax import lax
from jax.experimental import pallas as pl
from jax.experimental.pallas import tpu as pltpu

F32 = jnp.float32
BF16 = jnp.bfloat16

D_MODEL = 2048
CONV_CH = 1024
CONV_GROUP = 128
CONV_K = 31
MLA_HEADS = 8
QK_NOPE = 128
QK_ROPE = 64
V_HEAD = 128
Q_RANK = 512
KV_RANK = 512
ROPE_BASE = 10000.0
MLA_WIDTH = MLA_HEADS * V_HEAD
QK_PAD = 256
HGRN_HEADS = 16
HGRN_K = 128
HGRN_V = 128
EPS = 1e-6

LANE = 128
HALO = 32
VMEM_LIMIT = 56 * 1024 * 1024

EV_AV, EV_AG, EV_AZ = 0, 1024, 2048
EV_CQ, EV_CKV, EV_BZ, EV_KPE = 3072, 3584, 4096, 5120
EV_IN_PAD = 5376

HGRN_CHUNK = 128
HGRN_LEVELS = 7


def _params(*sem):
    return pltpu.CompilerParams(dimension_semantics=sem, vmem_limit_bytes=VMEM_LIMIT)


def _sigmoid(x):
    return 1.0 / (1.0 + jnp.exp(-x))


def _silu(x):
    return x * _sigmoid(x)


def _norm_matmul_kernel(x_ref, g_ref, w_ref, o_ref, h_ref):
    @pl.when(pl.program_id(1) == 0)
    def _():
        x = x_ref[...]
        ms = jnp.mean(x * x, axis=-1, keepdims=True)
        h_ref[...] = (x * lax.rsqrt(ms + EPS) * g_ref[...]).astype(BF16)

    o_ref[...] = jnp.dot(h_ref[...], w_ref[...], preferred_element_type=F32).astype(o_ref.dtype)


def _norm_matmul(x, g, w, *, tm, tn, name):
    m, d = x.shape
    n = w.shape[1]
    return pl.pallas_call(
        _norm_matmul_kernel,
        out_shape=jax.ShapeDtypeStruct((m, n), F32),
        grid=(m // tm, n // tn),
        in_specs=[
            pl.BlockSpec((tm, d), lambda i, j: (i, 0)),
            pl.BlockSpec((1, d), lambda i, j: (0, 0)),
            pl.BlockSpec((d, tn), lambda i, j: (0, j)),
        ],
        out_specs=pl.BlockSpec((tm, tn), lambda i, j: (i, j)),
        scratch_shapes=[pltpu.VMEM((tm, d), BF16)],
        compiler_params=_params("parallel", "arbitrary"),
        name=name,
    )(x, g.reshape(1, d), w)


def _conv_kernel(av_ref, ag_ref, az_ref, avh_ref, agh_ref, w_ref, cb_ref, lg_ref, lb_ref,
                 o_ref, u_ref, *, ts, tc, rows):
    si = pl.program_id(1)
    u_ref[HALO:, :] = av_ref[...] * _sigmoid(ag_ref[...])
    uh = avh_ref[...] * _sigmoid(agh_ref[...])
    u_ref[:HALO, :] = jnp.where(si > 0, uh, 0.0)

    w = w_ref[...]
    off = HALO - (CONV_K - 1)
    for r0 in range(0, ts, rows):
        acc = jnp.zeros((rows, tc), F32)
        for k in range(CONV_K):
            acc = acc + w[k:k + 1, :] * u_ref[pl.ds(r0 + off + k, rows), :]
        y = acc + cb_ref[...]
        outs = []
        for c0 in range(0, tc, CONV_GROUP):
            yg = y[:, c0:c0 + CONV_GROUP]
            mu = jnp.mean(yg, axis=-1, keepdims=True)
            d = yg - mu
            var = jnp.mean(d * d, axis=-1, keepdims=True)
            outs.append(d * lax.rsqrt(var + EPS))
        yn = jnp.concatenate(outs, axis=-1) if len(outs) > 1 else outs[0]
        yn = yn * lg_ref[...] + lb_ref[...]
        z = az_ref[pl.ds(r0, rows), :]
        o_ref[pl.ds(r0, rows), :] = (_silu(yn) * _silu(z)).astype(o_ref.dtype)


def _conv_branch(p, conv_w, conv_b, ln_g, ln_b, *, batch, seq, ts=256, tc=256, rows=32):
    nsb = seq // ts
    ncb = CONV_CH // tc
    hb = ts // HALO

    def cur(col0):
        return pl.BlockSpec((ts, tc), lambda b, s, c: (b * nsb + s, col0 // tc + c))

    def halo(col0):
        return pl.BlockSpec(
            (HALO, tc), lambda b, s, c: (jnp.maximum((b * nsb + s) * hb - 1, 0), col0 // tc + c))

    def vec():
        return pl.BlockSpec((1, tc), lambda b, s, c: (0, c))

    return pl.pallas_call(
        functools.partial(_conv_kernel, ts=ts, tc=tc, rows=rows),
        out_shape=jax.ShapeDtypeStruct((batch * seq, CONV_CH), BF16),
        grid=(batch, nsb, ncb),
        in_specs=[cur(EV_AV), cur(EV_AG), cur(EV_AZ), halo(EV_AV), halo(EV_AG),
                  pl.BlockSpec((CONV_K, tc), lambda b, s, c: (0, c)),
                  vec(), vec(), vec()],
        out_specs=pl.BlockSpec((ts, tc), lambda b, s, c: (b * nsb + s, c)),
        scratch_shapes=[pltpu.VMEM((HALO + ts, tc), F32)],
        compiler_params=_params("parallel", "parallel", "parallel"),
        name="conv_branch",
    )(p, p, p, p, p, conv_w, conv_b.reshape(1, -1), ln_g.reshape(1, -1), ln_b.reshape(1, -1))


def _rope128(x, cos, sin_lo, sin_hi):
    half = QK_ROPE // 2
    return (x * cos + pltpu.roll(x, LANE - half, axis=1) * sin_lo
            + pltpu.roll(x, half, axis=1) * sin_hi)


def _rms(x, g):
    ms = jnp.mean(x * x, axis=-1, keepdims=True)
    return x * lax.rsqrt(ms + EPS) * g


def _mla_proj_kernel(cq_ref, ckv_ref, kpe_ref, qg_ref, kvg_ref, wq_ref, wkv_ref,
                     cos_ref, slo_ref, shi_ref, q_ref, kn_ref, v_ref, kp_ref, *, scale):
    cos, slo, shi = cos_ref[...], slo_ref[...], shi_ref[...]
    hq = _rms(cq_ref[...], qg_ref[...]).astype(BF16)
    q = jnp.dot(hq, wq_ref[...], preferred_element_type=F32) * scale
    for h in range(MLA_HEADS):
        c0 = h * QK_PAD
        q_ref[:, c0:c0 + QK_NOPE] = q[:, c0:c0 + QK_NOPE].astype(BF16)
        pe = q[:, c0 + QK_NOPE:c0 + QK_PAD]
        q_ref[:, c0 + QK_NOPE:c0 + QK_PAD] = _rope128(pe, cos, slo, shi).astype(BF16)
    hkv = _rms(ckv_ref[...], kvg_ref[...]).astype(BF16)
    kv = jnp.dot(hkv, wkv_ref[...], preferred_element_type=F32)
    kn_ref[...] = kv[:, :MLA_WIDTH].astype(BF16)
    v_ref[...] = kv[:, MLA_WIDTH:].astype(BF16)
    kp_ref[...] = _rope128(kpe_ref[...], cos, slo, shi).astype(BF16)


def _mla_proj(p, q_norm_g, kv_norm_g, wq, wkv, cos, slo, shi, *, seq, tm=512):
    t = p.shape[0]
    nsb = seq // tm
    scale = float((QK_NOPE + QK_ROPE) ** -0.5)

    def rows(width, col0):
        return pl.BlockSpec((tm, width), lambda i: (i, col0 // width))

    def full(a):
        return pl.BlockSpec(a.shape, lambda i: (0, 0))

    def table():
        return pl.BlockSpec((tm, LANE), lambda i: (i % nsb, 0))

    qg = q_norm_g.reshape(1, -1)
    kvg = kv_norm_g.reshape(1, -1)
    return pl.pallas_call(
        functools.partial(_mla_proj_kernel, scale=scale),
        out_shape=(jax.ShapeDtypeStruct((t, MLA_HEADS * QK_PAD), BF16),
                   jax.ShapeDtypeStruct((t, MLA_WIDTH), BF16),
                   jax.ShapeDtypeStruct((t, MLA_WIDTH), BF16),
                   jax.ShapeDtypeStruct((t, LANE), BF16)),
        grid=(t // tm,),
        in_specs=[rows(Q_RANK, EV_CQ), rows(KV_RANK, EV_CKV), rows(LANE, EV_KPE),
                  full(qg), full(kvg), full(wq), full(wkv), table(), table(), table()],
        out_specs=(pl.BlockSpec((tm, MLA_HEADS * QK_PAD), lambda i: (i, 0)),
                   pl.BlockSpec((tm, MLA_WIDTH), lambda i: (i, 0)),
                   pl.BlockSpec((tm, MLA_WIDTH), lambda i: (i, 0)),
                   pl.BlockSpec((tm, LANE), lambda i: (i, 0))),
        compiler_params=_params("parallel"),
        name="mla_proj",
    )(p, p, p, qg, kvg, wq, wkv, cos, slo, shi)


def _attn_kernel(q_ref, kn_ref, kp_ref, v_ref, bz_ref, o_ref, kf_ref, m_ref, l_ref, acc_ref, *, tq):
    qi = pl.program_id(2)

    @pl.when(qi == 0)
    def _():
        kf_ref[:, :QK_NOPE] = kn_ref[...]
        kf_ref[:, QK_NOPE:] = kp_ref[...]

    q = q_ref[...]
    m_ref[...] = jnp.full(m_ref.shape, -jnp.inf, F32)
    l_ref[...] = jnp.zeros(l_ref.shape, F32)
    acc_ref[...] = jnp.zeros(acc_ref.shape, F32)

    def block(j, masked):
        start = pl.multiple_of(j * tq, tq)
        k = kf_ref[pl.ds(start, tq), :]
        s = lax.dot_general(q, k, (((1,), (1,)), ((), ())), preferred_element_type=F32)
        if masked:
            row = lax.broadcasted_iota(jnp.int32, s.shape, 0)
            col = lax.broadcasted_iota(jnp.int32, s.shape, 1)
            s = jnp.where(col <= row, s, -jnp.inf)
        m_prev = m_ref[...]
        m_new = jnp.maximum(m_prev, jnp.max(s, axis=-1, keepdims=True))
        alpha = jnp.exp(m_prev - m_new)
        p = jnp.exp(s - m_new[:, :1])
        l_ref[...] = alpha * l_ref[...] + jnp.sum(p, axis=-1, keepdims=True)
        pv = jnp.dot(p.astype(BF16), v_ref[pl.ds(start, tq), :], preferred_element_type=F32)
        acc_ref[...] = acc_ref[...] * alpha + pv
        m_ref[...] = m_new

    def body(j, carry):
        block(j, False)
        return carry

    lax.fori_loop(0, qi, body, 0)
    block(qi, True)
    o = acc_ref[...] / l_ref[...]
    o_ref[...] = (o * _silu(bz_ref[...])).astype(o_ref.dtype)


def _attention(q, kn, kp, v, p, *, batch, seq, tq=256):
    nq = seq // tq
    return pl.pallas_call(
        functools.partial(_attn_kernel, tq=tq),
        out_shape=jax.ShapeDtypeStruct((batch * seq, MLA_WIDTH), BF16),
        grid=(batch, MLA_HEADS, nq),
        in_specs=[
            pl.BlockSpec((tq, QK_PAD), lambda b, h, i: (b * nq + i, h)),
            pl.BlockSpec((seq, QK_NOPE), lambda b, h, i: (b, h)),
            pl.BlockSpec((seq, LANE), lambda b, h, i: (b, 0)),
            pl.BlockSpec((seq, V_HEAD), lambda b, h, i: (b, h)),
            pl.BlockSpec((tq, V_HEAD), lambda b, h, i: (b * nq + i, EV_BZ // V_HEAD + h)),
        ],
        out_specs=pl.BlockSpec((tq, V_HEAD), lambda b, h, i: (b * nq + i, h)),
        scratch_shapes=[pltpu.VMEM((seq, QK_PAD), BF16),
                        pltpu.VMEM((tq, LANE), F32),
                        pltpu.VMEM((tq, LANE), F32),
                        pltpu.VMEM((tq, V_HEAD), F32)],
        compiler_params=_params("parallel", "parallel", "arbitrary"),
        name="mla_attention",
    )(q, kn, kp, v, p)


def _out_proj_kernel(*refs, n_in, final_norm):
    a_refs = refs[:n_in]
    w_refs = refs[n_in:2 * n_in]
    x_ref = refs[2 * n_in]
    rest = refs[2 * n_in + 1:]
    y = x_ref[...]
    for a_ref, w_ref in zip(a_refs, w_refs):
        y = y + jnp.dot(a_ref[...], w_ref[...], preferred_element_type=F32)
    if final_norm:
        g_ref, o_ref = rest
        y = _rms(y, g_ref[...])
    else:
        (o_ref,) = rest
    o_ref[...] = y


def _out_proj(acts, w, x, final_g=None, *, tm=512, name):
    m, n = x.shape
    n_in = len(acts)
    kw = w.shape[0] // n_in
    in_specs = [pl.BlockSpec((tm, kw), lambda i: (i, 0)) for _ in acts]
    in_specs += [pl.BlockSpec((kw, n), (lambda i, c=c: (c, 0))) for c in range(n_in)]
    in_specs += [pl.BlockSpec((tm, n), lambda i: (i, 0))]
    args = list(acts) + [w] * n_in + [x]
    if final_g is not None:
        in_specs += [pl.BlockSpec((1, n), lambda i: (0, 0))]
        args += [final_g.reshape(1, n)]
    return pl.pallas_call(
        functools.partial(_out_proj_kernel, n_in=n_in, final_norm=final_g is not None),
        out_shape=jax.ShapeDtypeStruct((m, n), F32),
        grid=(m // tm,),
        in_specs=in_specs,
        out_specs=pl.BlockSpec((tm, n), lambda i: (i, 0)),
        compiler_params=_params("parallel"),
        name=name,
    )(*args)


def _hgrn_consts():
    c = HGRN_CHUNK
    t = np.arange(c)[:, None]
    s = np.arange(c)[None, :]
    tril = (s <= t).astype(np.float32)
    masks = []
    for lvl in range(HGRN_LEVELS):
        m = 1 << lvl
        same = (t // (2 * m)) == (s // (2 * m))
        masks.append((same & (t % (2 * m) >= m) & (s % (2 * m) < m)).astype(np.float32))
    rows = np.arange(c)[:, None] + np.zeros((1, LANE), np.int64)
    later = np.stack([((rows >> lvl) & 1).astype(np.float32) for lvl in range(3)])
    return tril, np.stack(masks), later


def _shift_rows(x, k):
    return pltpu.roll(x, (-k) % x.shape[0], axis=0)


def _hgrn_kernel(q_ref, f_ref, v_ref, g_ref, lbl_ref, ng_ref, tril_ref, mask_ref, later_ref,
                 o_ref, st_ref, *, heads, layer):
    c = HGRN_CHUNK

    @pl.when(pl.program_id(2) == 0)
    def _():
        st_ref[...] = jnp.zeros(st_ref.shape, F32)

    tril = tril_ref[...]
    nt = (((1,), (1,)), ((), ()))
    tn = (((0,), (0,)), ((), ()))

    for h in range(heads):
        cs = slice(h * HGRN_K, (h + 1) * HGRN_K)
        logits = lbl_ref[:, cs]
        e = jnp.exp(logits - jnp.max(logits, axis=0, keepdims=True))
        sm = e / jnp.sum(e, axis=0, keepdims=True)
        lb = jnp.sum(sm[:layer + 1], axis=0, keepdims=True) - sm[0:1]

        q = q_ref[:, cs]
        v = v_ref[:, cs]
        sig = _sigmoid(f_ref[:, cs])
        f = lb + (1.0 - lb) * sig
        k = (1.0 - lb) * (1.0 - sig)
        g2 = jnp.log2(f)

        g_hi = g2.astype(BF16)
        g_lo = (g2 - g_hi.astype(F32)).astype(BF16)
        b = (jnp.dot(tril, g_hi, preferred_element_type=F32)
             + jnp.dot(tril, g_lo, preferred_element_type=F32))
        b_last = b[c - 1:c, :]

        vb = v.astype(BF16)
        st = st_ref[h]
        qs = (q * jnp.exp2(b)).astype(BF16)
        o = lax.dot_general(qs, st.astype(BF16), nt, preferred_element_type=F32)
        o = o + jnp.sum(q * k, axis=-1, keepdims=True) * v

        ends = [b]
        for lvl in range(3):
            m = 1 << lvl
            prev = ends[-1]
            ends.append(jnp.where(later_ref[lvl] > 0.5, prev, _shift_rows(prev, m)))
        for lvl in range(3, HGRN_LEVELS - 1):
            m2 = 2 << lvl
            prev = ends[-1]
            tiles = [prev[(i // m2 + 1) * m2 - 8:(i // m2 + 1) * m2, :] for i in range(0, c, 8)]
            ends.append(jnp.concatenate(tiles, axis=0))

        a = jnp.zeros((c, c), F32)
        for lvl in range(HGRN_LEVELS):
            m = 1 << lvl
            end = ends[lvl]
            if m < 8:
                ref_q = _shift_rows(end, -m)
            else:
                ref_q = jnp.concatenate([jnp.zeros((m, HGRN_K), F32), end[:c - m, :]], axis=0)
            qh = (q * jnp.exp2(jnp.minimum(b - ref_q, 0.0))).astype(BF16)
            kh = (k * jnp.exp2(end - b)).astype(BF16)
            a = a + mask_ref[lvl] * lax.dot_general(qh, kh, nt, preferred_element_type=F32)
        o = o + jnp.dot(a.astype(BF16), vb, preferred_element_type=F32)

        ks = (k * jnp.exp2(b_last - b)).astype(BF16)
        st_ref[h] = st * jnp.exp2(b_last) + lax.dot_general(vb, ks, tn, preferred_element_type=F32)

        ms = jnp.mean(o * o, axis=-1, keepdims=True)
        on = o * lax.rsqrt(ms + EPS) * ng_ref[...]
        o_ref[:, cs] = (on * _silu(g_ref[:, cs])).astype(o_ref.dtype)


def _hgrn(p, lb_logits, norm_g, *, batch, seq, layer, heads=4):
    c = HGRN_CHUNK
    nc = seq // c
    width = heads * HGRN_K
    fdim = HGRN_HEADS * HGRN_K
    nhb = HGRN_HEADS // heads
    tril, masks, later = _hgrn_consts()

    def cols(col0):
        return pl.BlockSpec((c, width), lambda b, h, i: (b * nc + i, col0 // width + h))

    def const(a):
        nd = a.ndim
        return pl.BlockSpec(a.shape, lambda b, h, i: (0,) * nd)

    depth = lb_logits.shape[0]
    consts = (jnp.asarray(tril, BF16), jnp.asarray(masks), jnp.asarray(later))
    ng = norm_g.reshape(1, HGRN_V)
    return pl.pallas_call(
        functools.partial(_hgrn_kernel, heads=heads, layer=layer),
        out_shape=jax.ShapeDtypeStruct((batch * seq, D_MODEL), BF16),
        grid=(batch, nhb, nc),
        in_specs=[cols(0), cols(fdim), cols(2 * fdim), cols(2 * fdim + D_MODEL),
                  pl.BlockSpec((depth, width), lambda b, h, i: (0, h)),
                  const(ng), const(consts[0]), const(consts[1]), const(consts[2])],
        out_specs=pl.BlockSpec((c, width), lambda b, h, i: (b * nc + i, h)),
        scratch_shapes=[pltpu.VMEM((heads, HGRN_V, HGRN_K), F32)],
        compiler_params=_params("parallel", "parallel", "arbitrary"),
        name="hgrn2",
    )(p, p, p, p, lb_logits, ng, *consts)


def _rope_tables(seq):
    half = QK_ROPE // 2
    inv_freq = 1.0 / (ROPE_BASE ** (jnp.arange(0, QK_ROPE, 2, dtype=F32) / QK_ROPE))
    ang = jnp.arange(seq, dtype=F32)[:, None] * inv_freq[None, :]
    cos, sin = jnp.cos(ang), jnp.sin(ang)
    z = jnp.zeros((seq, half), F32)
    zz = jnp.zeros((seq, LANE - QK_ROPE), F32)
    cos_t = jnp.concatenate([cos, cos, zz], axis=1)
    sin_lo = jnp.concatenate([-sin, z, zz], axis=1)
    sin_hi = jnp.concatenate([z, sin, zz], axis=1)
    return cos_t, sin_lo, sin_hi


def _even_in_weight(w):
    kpe0 = 3 * CONV_CH + Q_RANK + KV_RANK
    d = w.shape[0]
    pad = jnp.zeros((d, EV_IN_PAD - (EV_KPE + QK_ROPE)), w.dtype)
    return jnp.concatenate(
        [w[:, :kpe0], w[:, kpe0 + QK_ROPE:], w[:, kpe0:kpe0 + QK_ROPE], pad], axis=1).astype(BF16)


def _even_layer(x, norm_g, w_in, conv_w, conv_b, ln_g, ln_b, q_norm_g, w_uq, kv_norm_g, w_ukv,
                w_out, tables, *, batch, seq):
    p = _norm_matmul(x, norm_g, _even_in_weight(w_in), tm=512, tn=1792, name="even_in_proj")
    a_out = _conv_branch(p, conv_w, conv_b, ln_g, ln_b, batch=batch, seq=seq)
    wq = jnp.pad(w_uq, ((0, 0), (0, 0), (0, QK_PAD - QK_NOPE - QK_ROPE)))
    wq = wq.reshape(Q_RANK, MLA_HEADS * QK_PAD).astype(BF16)
    wkv = jnp.concatenate([w_ukv[:, :, :QK_NOPE].reshape(KV_RANK, MLA_WIDTH),
                           w_ukv[:, :, QK_NOPE:].reshape(KV_RANK, MLA_WIDTH)], axis=1).astype(BF16)
    q, kn, v, kp = _mla_proj(p, q_norm_g, kv_norm_g, wq, wkv, *tables, seq=seq)
    b_out = _attention(q, kn, kp, v, p, batch=batch, seq=seq)
    return _out_proj([a_out, b_out], w_out.astype(BF16), x, name="even_out_proj")


def _odd_layer(x, norm_g, w_in, lb_logits, hgrn_norm_g, w_out, final_g, *, batch, seq, layer):
    p = _norm_matmul(x, norm_g, w_in.astype(BF16), tm=512, tn=2048, name="odd_in_proj")
    o = _hgrn(p, lb_logits, hgrn_norm_g, batch=batch, seq=seq, layer=layer)
    return _out_proj([o], w_out.astype(BF16), x, final_g, name="odd_out_proj")


def kernel(x, ev_norm_g, ev_w_in, conv_w, conv_b, conv_ln_g, conv_ln_b, mla_q_norm_g, mla_w_uq,
           mla_kv_norm_g, mla_w_ukv, ev_w_out, od_norm_g, od_w_in, hgrn_lb_logits, hgrn_norm_g,
           od_w_out, final_norm_g):
    batch, seq, d = x.shape
    depth = hgrn_lb_logits.shape[0]
    tables = _rope_tables(seq)
    y = x.reshape(batch * seq, d)
    for l in range(depth):
        j = l // 2
        last = l == depth - 1
        if l % 2 == 0:
            y = _even_layer(y, ev_norm_g[j], ev_w_in[j], conv_w[j], conv_b[j], conv_ln_g[j],
                            conv_ln_b[j], mla_q_norm_g[j], mla_w_uq[j], mla_kv_norm_g[j],
                            mla_w_ukv[j], ev_w_out[j], tables, batch=batch, seq=seq)
            assert not last
        else:
            y = _odd_layer(y, od_norm_g[j], od_w_in[j], hgrn_lb_logits, hgrn_norm_g[j], od_w_out[j],
                           final_norm_g if last else None, batch=batch, seq=seq, layer=l)
    return y.reshape(batch, seq, d)
```

```python
import functools
import math

import numpy as np
import jax
import jax.numpy as jnp
from jax import lax
from jax.experimental import pallas as pl
from jax.experimental.pallas import tpu as pltpu

F32 = jnp.float32
BF16 = jnp.bfloat16

D_MODEL = 2048
CONV_CH = 1024
CONV_GROUP = 128
CONV_K = 31
MLA_HEADS = 8
QK_NOPE = 128
QK_ROPE = 64
V_HEAD = 128
Q_RANK = 512
KV_RANK = 512
ROPE_BASE = 10000.0
MLA_WIDTH = MLA_HEADS * V_HEAD
QK_PAD = 256
HGRN_HEADS = 16
HGRN_K = 128
HGRN_V = 128
EPS = 1e-6

LANE = 128
SUB = 8
HALO = 32
VMEM_LIMIT = 56 * 1024 * 1024

EV_AV, EV_AG, EV_AZ = 0, 1024, 2048
EV_CQ, EV_CKV, EV_BZ, EV_KPE = 3072, 3584, 4096, 5120
EV_IN_PAD = 5376

HGRN_CHUNK = 128
HGRN_LEVELS = 7


def _params(*sem):
    return pltpu.CompilerParams(dimension_semantics=sem, vmem_limit_bytes=VMEM_LIMIT)


def _sigmoid(x):
    return 1.0 / (1.0 + jnp.exp(-x))


def _silu(x):
    return x * _sigmoid(x)


def _norm_matmul_kernel(x_ref, g_ref, w_ref, o_ref, h_ref):
    @pl.when(pl.program_id(1) == 0)
    def _():
        x = x_ref[...]
        ms = jnp.mean(x * x, axis=-1, keepdims=True)
        h_ref[...] = (x * lax.rsqrt(ms + EPS) * g_ref[...]).astype(BF16)

    o_ref[...] = jnp.dot(h_ref[...], w_ref[...], preferred_element_type=F32).astype(o_ref.dtype)


def _norm_matmul(x, g, w, *, tm, tn, name):
    m, d = x.shape
    n = w.shape[1]
    return pl.pallas_call(
        _norm_matmul_kernel,
        out_shape=jax.ShapeDtypeStruct((m, n), F32),
        grid=(m // tm, n // tn),
        in_specs=[
            pl.BlockSpec((tm, d), lambda i, j: (i, 0)),
            pl.BlockSpec((1, d), lambda i, j: (0, 0)),
            pl.BlockSpec((d, tn), lambda i, j: (0, j)),
        ],
        out_specs=pl.BlockSpec((tm, tn), lambda i, j: (i, j)),
        scratch_shapes=[pltpu.VMEM((tm, d), BF16)],
        compiler_params=_params("parallel", "arbitrary"),
        name=name,
    )(x, g.reshape(1, d), w)


def _conv_kernel(av_ref, ag_ref, az_ref, avh_ref, agh_ref, w_ref, cb_ref, lg_ref, lb_ref,
                 o_ref, u_ref, *, ts, tc, rows):
    si = pl.program_id(1)
    u_ref[HALO:, :] = av_ref[...] * _sigmoid(ag_ref[...])
    uh = avh_ref[...] * _sigmoid(agh_ref[...])
    u_ref[:HALO, :] = jnp.where(si > 0, uh, 0.0)

    w = w_ref[...]
    off = HALO - (CONV_K - 1)
    for r0 in range(0, ts, rows):
        y = cb_ref[...]
        for res in range(SUB):
            taps = [k for k in range(CONV_K) if (off + k) % SUB == res]
            n = rows + (SUB if res else 0)
            z = None
            for k in taps:
                term = w[k:k + 1, :] * u_ref[pl.ds(r0 + off + k - res, n), :]
                z = term if z is None else z + term
            y = y + z[res:res + rows, :]
        outs = []
        for c0 in range(0, tc, CONV_GROUP):
            yg = y[:, c0:c0 + CONV_GROUP]
            mu = jnp.mean(yg, axis=-1, keepdims=True)
            d = yg - mu
            var = jnp.mean(d * d, axis=-1, keepdims=True)
            outs.append(d * lax.rsqrt(var + EPS))
        yn = jnp.concatenate(outs, axis=-1) if len(outs) > 1 else outs[0]
        yn = yn * lg_ref[...] + lb_ref[...]
        z = az_ref[pl.ds(r0, rows), :]
        o_ref[pl.ds(r0, rows), :] = (_silu(yn) * _silu(z)).astype(o_ref.dtype)


def _conv_branch(p, conv_w, conv_b, ln_g, ln_b, *, batch, seq, ts=256, tc=256, rows=32):
    nsb = seq // ts
    ncb = CONV_CH // tc
    hb = ts // HALO

    def cur(col0):
        return pl.BlockSpec((ts, tc), lambda b, s, c: (b * nsb + s, col0 // tc + c))

    def halo(col0):
        return pl.BlockSpec(
            (HALO, tc), lambda b, s, c: (jnp.maximum((b * nsb + s) * hb - 1, 0), col0 // tc + c))

    def vec():
        return pl.BlockSpec((1, tc), lambda b, s, c: (0, c))

    return pl.pallas_call(
        functools.partial(_conv_kernel, ts=ts, tc=tc, rows=rows),
        out_shape=jax.ShapeDtypeStruct((batch * seq, CONV_CH), BF16),
        grid=(batch, nsb, ncb),
        in_specs=[cur(EV_AV), cur(EV_AG), cur(EV_AZ), halo(EV_AV), halo(EV_AG),
                  pl.BlockSpec((CONV_K, tc), lambda b, s, c: (0, c)),
                  vec(), vec(), vec()],
        out_specs=pl.BlockSpec((ts, tc), lambda b, s, c: (b * nsb + s, c)),
        scratch_shapes=[pltpu.VMEM((HALO + ts, tc), F32)],
        compiler_params=_params("parallel", "parallel", "parallel"),
        name="conv_branch",
    )(p, p, p, p, p, conv_w, conv_b.reshape(1, -1), ln_g.reshape(1, -1), ln_b.reshape(1, -1))


def _rope128(x, cos, sin_lo, sin_hi):
    half = QK_ROPE // 2
    return (x * cos + pltpu.roll(x, LANE - half, axis=1) * sin_lo
            + pltpu.roll(x, half, axis=1) * sin_hi)


def _rms(x, g):
    ms = jnp.mean(x * x, axis=-1, keepdims=True)
    return x * lax.rsqrt(ms + EPS) * g


def _mla_proj_kernel(cq_ref, ckv_ref, kpe_ref, qg_ref, kvg_ref, wq_ref, wkv_ref,
                     cos_ref, slo_ref, shi_ref, q_ref, kn_ref, v_ref, kp_ref, *, scale):
    cos, slo, shi = cos_ref[...], slo_ref[...], shi_ref[...]
    hq = _rms(cq_ref[...], qg_ref[...]).astype(BF16)
    q = jnp.dot(hq, wq_ref[...], preferred_element_type=F32) * scale
    for h in range(MLA_HEADS):
        c0 = h * QK_PAD
        q_ref[:, c0:c0 + QK_NOPE] = q[:, c0:c0 + QK_NOPE].astype(BF16)
        pe = q[:, c0 + QK_NOPE:c0 + QK_PAD]
        q_ref[:, c0 + QK_NOPE:c0 + QK_PAD] = _rope128(pe, cos, slo, shi).astype(BF16)
    hkv = _rms(ckv_ref[...], kvg_ref[...]).astype(BF16)
    kv = jnp.dot(hkv, wkv_ref[...], preferred_element_type=F32)
    kn_ref[...] = kv[:, :MLA_WIDTH].astype(BF16)
    v_ref[...] = kv[:, MLA_WIDTH:].astype(BF16)
    kp_ref[...] = _rope128(kpe_ref[...], cos, slo, shi).astype(BF16)


def _mla_proj(p, q_norm_g, kv_norm_g, wq, wkv, cos, slo, shi, *, seq, tm=512):
    t = p.shape[0]
    nsb = seq // tm
    scale = float((QK_NOPE + QK_ROPE) ** -0.5 * math.log2(math.e))

    def rows(width, col0):
        return pl.BlockSpec((tm, width), lambda i: (i, col0 // width))

    def full(a):
        return pl.BlockSpec(a.shape, lambda i: (0, 0))

    def table():
        return pl.BlockSpec((tm, LANE), lambda i: (i % nsb, 0))

    qg = q_norm_g.reshape(1, -1)
    kvg = kv_norm_g.reshape(1, -1)
    return pl.pallas_call(
        functools.partial(_mla_proj_kernel, scale=scale),
        out_shape=(jax.ShapeDtypeStruct((t, MLA_HEADS * QK_PAD), BF16),
                   jax.ShapeDtypeStruct((t, MLA_WIDTH), BF16),
                   jax.ShapeDtypeStruct((t, MLA_WIDTH), BF16),
                   jax.ShapeDtypeStruct((t, LANE), BF16)),
        grid=(t // tm,),
        in_specs=[rows(Q_RANK, EV_CQ), rows(KV_RANK, EV_CKV), rows(LANE, EV_KPE),
                  full(qg), full(kvg), full(wq), full(wkv), table(), table(), table()],
        out_specs=(pl.BlockSpec((tm, MLA_HEADS * QK_PAD), lambda i: (i, 0)),
                   pl.BlockSpec((tm, MLA_WIDTH), lambda i: (i, 0)),
                   pl.BlockSpec((tm, MLA_WIDTH), lambda i: (i, 0)),
                   pl.BlockSpec((tm, LANE), lambda i: (i, 0))),
        compiler_params=_params("parallel"),
        name="mla_proj",
    )(p, p, p, qg, kvg, wq, wkv, cos, slo, shi)


def _attn_kernel(q_ref, kn_ref, kp_ref, v_ref, bz_ref, o_ref, kf_ref, vf_ref, m_ref, acc_ref,
                 s_ref, *, tq, heads):
    qi = pl.program_id(2)

    @pl.when(qi == 0)
    def _():
        for h in range(heads):
            kf_ref[h, :, :QK_NOPE] = kn_ref[:, h * QK_NOPE:(h + 1) * QK_NOPE]
            kf_ref[h, :, QK_NOPE:] = kp_ref[...]
            vf_ref[h, :, :V_HEAD] = v_ref[:, h * V_HEAD:(h + 1) * V_HEAD]
            vf_ref[h, :, V_HEAD:] = jnp.ones((vf_ref.shape[1], LANE), BF16)

    m_ref[...] = jnp.full(m_ref.shape, -jnp.inf, F32)
    acc_ref[...] = jnp.zeros(acc_ref.shape, F32)

    def scores(j, slot):
        start = pl.multiple_of(j * tq, tq)
        for h in range(heads):
            q = q_ref[:, h * QK_PAD:(h + 1) * QK_PAD]
            k = kf_ref[h, pl.ds(start, tq), :]
            s_ref[slot, h] = lax.dot_general(q, k, (((1,), (1,)), ((), ())),
                                             preferred_element_type=F32)

    def consume(j, slot, masked):
        start = pl.multiple_of(j * tq, tq)
        for h in range(heads):
            s = s_ref[slot, h]
            if masked:
                row = lax.broadcasted_iota(jnp.int32, s.shape, 0)
                col = lax.broadcasted_iota(jnp.int32, s.shape, 1)
                s = jnp.where(col <= row, s, -jnp.inf)
            m_prev = m_ref[h]
            m_new = jnp.maximum(m_prev, jnp.max(s, axis=-1, keepdims=True))
            alpha = jnp.exp2(m_prev - m_new)
            p = jnp.exp2(s - m_new[:, :1])
            pv = jnp.dot(p.astype(BF16), vf_ref[h, pl.ds(start, tq), :],
                         preferred_element_type=F32)
            acc_ref[h] = acc_ref[h] * jnp.concatenate([alpha, alpha], axis=1) + pv
            m_ref[h] = m_new

    scores(0, 0)
    pairs = qi // 2

    def body(t, carry):
        scores(2 * t + 1, 1)
        consume(2 * t, 0, False)
        scores(2 * t + 2, 0)
        consume(2 * t + 1, 1, False)
        return carry

    lax.fori_loop(0, pairs, body, 0)

    @pl.when(qi % 2 == 0)
    def _():
        consume(qi, 0, True)

    @pl.when(qi % 2 == 1)
    def _():
        scores(qi, 1)
        consume(qi - 1, 0, False)
        consume(qi, 1, True)

    for h in range(heads):
        cs = slice(h * V_HEAD, (h + 1) * V_HEAD)
        o = acc_ref[h, :, :V_HEAD] / acc_ref[h, :, V_HEAD:]
        o_ref[:, cs] = (o * _silu(bz_ref[:, cs])).astype(o_ref.dtype)


def _attention(q, kn, kp, v, p, *, batch, seq, tq=512, heads=2):
    nq = seq // tq
    vw = heads * V_HEAD
    return pl.pallas_call(
        functools.partial(_attn_kernel, tq=tq, heads=heads),
        out_shape=jax.ShapeDtypeStruct((batch * seq, MLA_WIDTH), BF16),
        grid=(batch, MLA_HEADS // heads, nq),
        in_specs=[
            pl.BlockSpec((tq, heads * QK_PAD), lambda b, h, i: (b * nq + i, h)),
            pl.BlockSpec((seq, heads * QK_NOPE), lambda b, h, i: (b, h)),
            pl.BlockSpec((seq, LANE), lambda b, h, i: (b, 0)),
            pl.BlockSpec((seq, vw), lambda b, h, i: (b, h)),
            pl.BlockSpec((tq, vw), lambda b, h, i: (b * nq + i, EV_BZ // vw + h)),
        ],
        out_specs=pl.BlockSpec((tq, vw), lambda b, h, i: (b * nq + i, h)),
        scratch_shapes=[pltpu.VMEM((heads, seq, QK_PAD), BF16),
                        pltpu.VMEM((heads, seq, V_HEAD + LANE), BF16),
                        pltpu.VMEM((heads, tq, LANE), F32),
                        pltpu.VMEM((heads, tq, V_HEAD + LANE), F32),
                        pltpu.VMEM((2, heads, tq, tq), F32)],
        compiler_params=_params("parallel", "parallel", "arbitrary"),
        name="mla_attention",
    )(q, kn, kp, v, p)


def _out_proj_kernel(*refs, n_in, final_norm):
    a_refs = refs[:n_in]
    w_refs = refs[n_in:2 * n_in]
    x_ref = refs[2 * n_in]
    rest = refs[2 * n_in + 1:]
    y = x_ref[...]
    for a_ref, w_ref in zip(a_refs, w_refs):
        y = y + jnp.dot(a_ref[...], w_ref[...], preferred_element_type=F32)
    if final_norm:
        g_ref, o_ref = rest
        y = _rms(y, g_ref[...])
    else:
        (o_ref,) = rest
    o_ref[...] = y


def _out_proj(acts, w, x, final_g=None, *, tm=512, name):
    m, n = x.shape
    n_in = len(acts)
    kw = w.shape[0] // n_in
    in_specs = [pl.BlockSpec((tm, kw), lambda i: (i, 0)) for _ in acts]
    in_specs += [pl.BlockSpec((kw, n), (lambda i, c=c: (c, 0))) for c in range(n_in)]
    in_specs += [pl.BlockSpec((tm, n), lambda i: (i, 0))]
    args = list(acts) + [w] * n_in + [x]
    if final_g is not None:
        in_specs += [pl.BlockSpec((1, n), lambda i: (0, 0))]
        args += [final_g.reshape(1, n)]
    return pl.pallas_call(
        functools.partial(_out_proj_kernel, n_in=n_in, final_norm=final_g is not None),
        out_shape=jax.ShapeDtypeStruct((m, n), F32),
        grid=(m // tm,),
        in_specs=in_specs,
        out_specs=pl.BlockSpec((tm, n), lambda i: (i, 0)),
        compiler_params=_params("parallel"),
        name=name,
    )(*args)


def _hgrn_consts():
    c = HGRN_CHUNK
    t = np.arange(c)[:, None]
    s = np.arange(c)[None, :]
    masks = []
    for lvl in range(HGRN_LEVELS):
        m = 1 << lvl
        same = (t // (2 * m)) == (s // (2 * m))
        masks.append((same & (t % (2 * m) >= m) & (s % (2 * m) < m)).astype(np.float32))
    return np.stack(masks)


def _hgrn_factors(f):
    nt = HGRN_CHUNK // SUB
    row = lax.broadcasted_iota(jnp.int32, (SUB, HGRN_K), 0)
    qf = [f[i * SUB:(i + 1) * SUB, :] for i in range(nt)]
    kf = None
    e = list(qf)
    levels = []
    m = 1
    while m < SUB:
        levels.append((qf, kf))
        later = (row & m) != 0
        nq, nk, ne = [], [], []
        for i in range(nt):
            down = pltpu.roll(e[i], m, axis=0)
            up = pltpu.roll(e[i], SUB - m, axis=0)
            nq.append(qf[i] * jnp.where(later, down, 1.0))
            ku = jnp.where(later, 1.0, up)
            nk.append(ku if kf is None else kf[i] * ku)
            ne.append(e[i] * jnp.where(later, down, up))
        qf, kf, e = nq, nk, ne
        m *= 2
    mt = 1
    while mt <= nt:
        levels.append((qf, kf))
        if mt == nt:
            break
        nq, nk, ne = [], [], []
        for i in range(nt):
            if (i // mt) % 2:
                nq.append(qf[i] * e[i - mt])
                nk.append(kf[i])
            else:
                nq.append(qf[i])
                nk.append(kf[i] * e[i + mt])
            first = (i // (2 * mt)) * 2 * mt
            ne.append(e[first] * e[first + mt] if i % (2 * mt) == 0 else None)
        for i in range(nt):
            if ne[i] is None:
                ne[i] = ne[(i // (2 * mt)) * 2 * mt]
        qf, kf, e = nq, nk, ne
        mt *= 2
    return levels[:-1], levels[-1], e[0]


def _hgrn_kernel(q_ref, f_ref, v_ref, g_ref, lbl_ref, ng_ref, mask_ref, o_ref, st_ref,
                 *, heads, layer):
    c = HGRN_CHUNK

    @pl.when(pl.program_id(2) == 0)
    def _():
        st_ref[...] = jnp.zeros(st_ref.shape, F32)

    nt = (((1,), (1,)), ((), ()))
    tn = (((0,), (0,)), ((), ()))

    def scaled(x, fac):
        y = x if fac is None else x * jnp.concatenate(fac, axis=0)
        return y.astype(BF16)

    for h in range(heads):
        cs = slice(h * HGRN_K, (h + 1) * HGRN_K)
        logits = lbl_ref[:, cs]
        e = jnp.exp(logits - jnp.max(logits, axis=0, keepdims=True))
        sm = e / jnp.sum(e, axis=0, keepdims=True)
        lb = jnp.sum(sm[:layer + 1], axis=0, keepdims=True) - sm[0:1]

        q = q_ref[:, cs]
        v = v_ref[:, cs]
        gate = (1.0 - lb) * _sigmoid(f_ref[:, cs])
        f = lb + gate
        k = (1.0 - lb) - gate
        levels, (qf_c, kf_c), e_c = _hgrn_factors(f)

        vb = v.astype(BF16)
        st = st_ref[h]
        a = jnp.zeros((c, c), F32)
        for lvl, (qf, kf) in enumerate(levels):
            pair = lax.dot_general(scaled(q, qf), scaled(k, kf), nt, preferred_element_type=F32)
            a = a + mask_ref[lvl] * pair
        o = (lax.dot_general(scaled(q, qf_c), st.astype(BF16), nt, preferred_element_type=F32)
             + jnp.dot(a.astype(BF16), vb, preferred_element_type=F32)
             + jnp.sum(q * k, axis=-1, keepdims=True) * v)

        st_ref[h] = st * e_c[0:1, :] + lax.dot_general(vb, scaled(k, kf_c), tn,
                                                       preferred_element_type=F32)

        ms = jnp.mean(o * o, axis=-1, keepdims=True)
        on = o * lax.rsqrt(ms + EPS) * ng_ref[...]
        o_ref[:, cs] = (on * _silu(g_ref[:, cs])).astype(o_ref.dtype)


def _hgrn(p, lb_logits, norm_g, *, batch, seq, layer, heads=4):
    c = HGRN_CHUNK
    nc = seq // c
    width = heads * HGRN_K
    fdim = HGRN_HEADS * HGRN_K
    nhb = HGRN_HEADS // heads
    masks = jnp.asarray(_hgrn_consts())

    def cols(col0):
        return pl.BlockSpec((c, width), lambda b, h, i: (b * nc + i, col0 // width + h))

    def const(a):
        nd = a.ndim
        return pl.BlockSpec(a.shape, lambda b, h, i: (0,) * nd)

    depth = lb_logits.shape[0]
    ng = norm_g.reshape(1, HGRN_V)
    return pl.pallas_call(
        functools.partial(_hgrn_kernel, heads=heads, layer=layer),
        out_shape=jax.ShapeDtypeStruct((batch * seq, D_MODEL), BF16),
        grid=(batch, nhb, nc),
        in_specs=[cols(0), cols(fdim), cols(2 * fdim), cols(2 * fdim + D_MODEL),
                  pl.BlockSpec((depth, width), lambda b, h, i: (0, h)),
                  const(ng), const(masks)],
        out_specs=pl.BlockSpec((c, width), lambda b, h, i: (b * nc + i, h)),
        scratch_shapes=[pltpu.VMEM((heads, HGRN_V, HGRN_K), F32)],
        compiler_params=_params("parallel", "parallel", "arbitrary"),
        name="hgrn2",
    )(p, p, p, p, lb_logits, ng, masks)


def _rope_tables(seq):
    half = QK_ROPE // 2
    inv_freq = 1.0 / (ROPE_BASE ** (jnp.arange(0, QK_ROPE, 2, dtype=F32) / QK_ROPE))
    ang = jnp.arange(seq, dtype=F32)[:, None] * inv_freq[None, :]
    cos, sin = jnp.cos(ang), jnp.sin(ang)
    z = jnp.zeros((seq, half), F32)
    zz = jnp.zeros((seq, LANE - QK_ROPE), F32)
    cos_t = jnp.concatenate([cos, cos, zz], axis=1)
    sin_lo = jnp.concatenate([-sin, z, zz], axis=1)
    sin_hi = jnp.concatenate([z, sin, zz], axis=1)
    return cos_t, sin_lo, sin_hi


def _even_in_weight(w):
    kpe0 = 3 * CONV_CH + Q_RANK + KV_RANK
    d = w.shape[0]
    pad = jnp.zeros((d, EV_IN_PAD - (EV_KPE + QK_ROPE)), w.dtype)
    return jnp.concatenate(
        [w[:, :kpe0], w[:, kpe0 + QK_ROPE:], w[:, kpe0:kpe0 + QK_ROPE], pad], axis=1).astype(BF16)


def _even_layer(x, norm_g, w_in, conv_w, conv_b, ln_g, ln_b, q_norm_g, w_uq, kv_norm_g, w_ukv,
                w_out, tables, *, batch, seq):
    p = _norm_matmul(x, norm_g, _even_in_weight(w_in), tm=512, tn=1792, name="even_in_proj")
    a_out = _conv_branch(p, conv_w, conv_b, ln_g, ln_b, batch=batch, seq=seq)
    wq = jnp.pad(w_uq, ((0, 0), (0, 0), (0, QK_PAD - QK_NOPE - QK_ROPE)))
    wq = wq.reshape(Q_RANK, MLA_HEADS * QK_PAD).astype(BF16)
    wkv = jnp.concatenate([w_ukv[:, :, :QK_NOPE].reshape(KV_RANK, MLA_WIDTH),
                           w_ukv[:, :, QK_NOPE:].reshape(KV_RANK, MLA_WIDTH)], axis=1).astype(BF16)
    q, kn, v, kp = _mla_proj(p, q_norm_g, kv_norm_g, wq, wkv, *tables, seq=seq)
    b_out = _attention(q, kn, kp, v, p, batch=batch, seq=seq)
    return _out_proj([a_out, b_out], w_out.astype(BF16), x, name="even_out_proj")


def _odd_layer(x, norm_g, w_in, lb_logits, hgrn_norm_g, w_out, final_g, *, batch, seq, layer):
    p = _norm_matmul(x, norm_g, w_in.astype(BF16), tm=512, tn=2048, name="odd_in_proj")
    o = _hgrn(p, lb_logits, hgrn_norm_g, batch=batch, seq=seq, layer=layer)
    return _out_proj([o], w_out.astype(BF16), x, final_g, name="odd_out_proj")


def kernel(x, ev_norm_g, ev_w_in, conv_w, conv_b, conv_ln_g, conv_ln_b, mla_q_norm_g, mla_w_uq,
           mla_kv_norm_g, mla_w_ukv, ev_w_out, od_norm_g, od_w_in, hgrn_lb_logits, hgrn_norm_g,
           od_w_out, final_norm_g):
    batch, seq, d = x.shape
    depth = hgrn_lb_logits.shape[0]
    tables = _rope_tables(seq)
    y = x.reshape(batch * seq, d)
    for l in range(depth):
        j = l // 2
        last = l == depth - 1
        if l % 2 == 0:
            y = _even_layer(y, ev_norm_g[j], ev_w_in[j], conv_w[j], conv_b[j], conv_ln_g[j],
                            conv_ln_b[j], mla_q_norm_g[j], mla_w_uq[j], mla_kv_norm_g[j],
                            mla_w_ukv[j], ev_w_out[j], tables, batch=batch, seq=seq)
            assert not last
        else:
            y = _odd_layer(y, od_norm_g[j], od_w_in[j], hgrn_lb_logits, hgrn_norm_g[j], od_w_out[j],
                           final_norm_g if last else None, batch=batch, seq=seq, layer=l)
    return y.reshape(batch, seq, d)
```

```python
import functools
import math

import numpy as np
import jax
import jax.numpy as jnp
from jax import lax
from jax.experimental import pallas as pl
from jax.experimental.pallas import tpu as pltpu

F32 = jnp.float32
BF16 = jnp.bfloat16

D_MODEL = 2048
CONV_CH = 1024
CONV_GROUP = 128
CONV_K = 31
MLA_HEADS = 8
QK_NOPE = 128
QK_ROPE = 64
V_HEAD = 128
Q_RANK = 512
KV_RANK = 512
ROPE_BASE = 10000.0
MLA_WIDTH = MLA_HEADS * V_HEAD
QK_PAD = 256
HGRN_HEADS = 16
HGRN_K = 128
HGRN_V = 128
EPS = 1e-6

LANE = 128
SUB = 8
HALO = 32
VMEM_LIMIT = 56 * 1024 * 1024

EV_AV, EV_AG, EV_AZ = 0, 1024, 2048
EV_CQ, EV_CKV, EV_BZ, EV_KPE = 3072, 3584, 4096, 5120
EV_IN_PAD = 5376

HGRN_CHUNK = 128
HGRN_LEVELS = 7


def _params(*sem):
    return pltpu.CompilerParams(dimension_semantics=sem, vmem_limit_bytes=VMEM_LIMIT)


def _sigmoid(x):
    return 1.0 / (1.0 + jnp.exp(-x))


def _silu(x):
    return x * _sigmoid(x)


def _rms(x, g):
    ms = jnp.mean(x * x, axis=-1, keepdims=True)
    return x * lax.rsqrt(ms + EPS) * g


def _dot(h, w, w_transposed):
    dims = (((1,), (1,)), ((), ())) if w_transposed else (((1,), (0,)), ((), ()))
    return lax.dot_general(h, w, dims, preferred_element_type=F32)


def _norm_matmul_kernel(x_ref, g_ref, w_ref, o_ref, h_ref, *, w_transposed):
    @pl.when(pl.program_id(1) == 0)
    def _():
        h_ref[...] = _rms(x_ref[...], g_ref[...]).astype(BF16)

    o_ref[...] = _dot(h_ref[...], w_ref[...], w_transposed)


def _matmul_kernel(h_ref, w_ref, o_ref, *, w_transposed):
    o_ref[...] = _dot(h_ref[...], w_ref[...], w_transposed)


def _in_proj(x, g, w, *, w_transposed=False, tm, tn, name):
    m, d = x.shape
    n = w.shape[0] if w_transposed else w.shape[1]
    w_spec = (pl.BlockSpec((tn, d), lambda i, j: (j, 0)) if w_transposed
              else pl.BlockSpec((d, tn), lambda i, j: (0, j)))
    in_specs = [pl.BlockSpec((tm, d), lambda i, j: (i, 0)), w_spec]
    args = [x, w]
    scratch = []
    body = _matmul_kernel
    if g is not None:
        in_specs.insert(1, pl.BlockSpec((1, d), lambda i, j: (0, 0)))
        args.insert(1, g.reshape(1, d))
        scratch = [pltpu.VMEM((tm, d), BF16)]
        body = _norm_matmul_kernel
    return pl.pallas_call(
        functools.partial(body, w_transposed=w_transposed),
        out_shape=jax.ShapeDtypeStruct((m, n), F32),
        grid=(m // tm, n // tn),
        in_specs=in_specs,
        out_specs=pl.BlockSpec((tm, tn), lambda i, j: (i, j)),
        scratch_shapes=scratch,
        compiler_params=_params("parallel", "arbitrary"),
        name=name,
    )(*args)


def _conv_kernel(av_ref, ag_ref, az_ref, avh_ref, agh_ref, w_ref, cb_ref, lg_ref, lb_ref,
                 o_ref, u_ref, *, ts, tc, rows):
    si = pl.program_id(1)
    u_ref[HALO:, :] = av_ref[...] * _sigmoid(ag_ref[...])
    uh = avh_ref[...] * _sigmoid(agh_ref[...])
    u_ref[:HALO, :] = jnp.where(si > 0, uh, 0.0)

    w = w_ref[...]
    off = HALO - (CONV_K - 1)
    for r0 in range(0, ts, rows):
        y = cb_ref[...]
        for res in range(SUB):
            taps = [k for k in range(CONV_K) if (off + k) % SUB == res]
            n = rows + (SUB if res else 0)
            z = None
            for k in taps:
                term = w[k:k + 1, :] * u_ref[pl.ds(r0 + off + k - res, n), :]
                z = term if z is None else z + term
            y = y + z[res:res + rows, :]
        outs = []
        for c0 in range(0, tc, CONV_GROUP):
            yg = y[:, c0:c0 + CONV_GROUP]
            mu = jnp.mean(yg, axis=-1, keepdims=True)
            d = yg - mu
            var = jnp.mean(d * d, axis=-1, keepdims=True)
            outs.append(d * lax.rsqrt(var + EPS))
        yn = jnp.concatenate(outs, axis=-1) if len(outs) > 1 else outs[0]
        yn = yn * lg_ref[...] + lb_ref[...]
        z = az_ref[pl.ds(r0, rows), :]
        o_ref[pl.ds(r0, rows), :] = (_silu(yn) * _silu(z)).astype(o_ref.dtype)


def _conv_branch(p, conv_w, conv_b, ln_g, ln_b, *, batch, seq, ts=256, tc=256, rows=32):
    nsb = seq // ts
    ncb = CONV_CH // tc
    hb = ts // HALO

    def cur(col0):
        return pl.BlockSpec((ts, tc), lambda b, s, c: (b * nsb + s, col0 // tc + c))

    def halo(col0):
        return pl.BlockSpec(
            (HALO, tc), lambda b, s, c: (jnp.maximum((b * nsb + s) * hb - 1, 0), col0 // tc + c))

    def vec():
        return pl.BlockSpec((1, tc), lambda b, s, c: (0, c))

    return pl.pallas_call(
        functools.partial(_conv_kernel, ts=ts, tc=tc, rows=rows),
        out_shape=jax.ShapeDtypeStruct((batch * seq, CONV_CH), BF16),
        grid=(batch, nsb, ncb),
        in_specs=[cur(EV_AV), cur(EV_AG), cur(EV_AZ), halo(EV_AV), halo(EV_AG),
                  pl.BlockSpec((CONV_K, tc), lambda b, s, c: (0, c)),
                  vec(), vec(), vec()],
        out_specs=pl.BlockSpec((ts, tc), lambda b, s, c: (b * nsb + s, c)),
        scratch_shapes=[pltpu.VMEM((HALO + ts, tc), F32)],
        compiler_params=_params("parallel", "parallel", "parallel"),
        name="conv_branch",
    )(p, p, p, p, p, conv_w, conv_b.reshape(1, -1), ln_g.reshape(1, -1), ln_b.reshape(1, -1))


def _rope128(x, cos, sin_lo, sin_hi):
    half = QK_ROPE // 2
    return (x * cos + pltpu.roll(x, LANE - half, axis=1) * sin_lo
            + pltpu.roll(x, half, axis=1) * sin_hi)


def _mla_proj_kernel(cq_ref, ckv_ref, kpe_ref, qg_ref, kvg_ref, wq_ref, wkv_ref,
                     cos_ref, slo_ref, shi_ref, q_ref, kn_ref, v_ref, kp_ref, *, scale):
    cos, slo, shi = cos_ref[...], slo_ref[...], shi_ref[...]
    hq = _rms(cq_ref[...], qg_ref[...]).astype(BF16)
    q = jnp.dot(hq, wq_ref[...], preferred_element_type=F32) * scale
    for h in range(MLA_HEADS):
        c0 = h * QK_PAD
        q_ref[:, c0:c0 + QK_NOPE] = q[:, c0:c0 + QK_NOPE].astype(BF16)
        pe = q[:, c0 + QK_NOPE:c0 + QK_PAD]
        q_ref[:, c0 + QK_NOPE:c0 + QK_PAD] = _rope128(pe, cos, slo, shi).astype(BF16)
    hkv = _rms(ckv_ref[...], kvg_ref[...]).astype(BF16)
    kv = jnp.dot(hkv, wkv_ref[...], preferred_element_type=F32)
    kn_ref[...] = kv[:, :MLA_WIDTH].astype(BF16)
    v_ref[...] = kv[:, MLA_WIDTH:].astype(BF16)
    kp_ref[...] = _rope128(kpe_ref[...], cos, slo, shi).astype(BF16)


def _mla_proj(p, q_norm_g, kv_norm_g, wq, wkv, cos, slo, shi, *, seq, tm=512):
    t = p.shape[0]
    nsb = seq // tm
    scale = float((QK_NOPE + QK_ROPE) ** -0.5 * math.log2(math.e))

    def rows(width, col0):
        return pl.BlockSpec((tm, width), lambda i: (i, col0 // width))

    def full(a):
        return pl.BlockSpec(a.shape, lambda i: (0, 0))

    def table():
        return pl.BlockSpec((tm, LANE), lambda i: (i % nsb, 0))

    qg = q_norm_g.reshape(1, -1)
    kvg = kv_norm_g.reshape(1, -1)
    return pl.pallas_call(
        functools.partial(_mla_proj_kernel, scale=scale),
        out_shape=(jax.ShapeDtypeStruct((t, MLA_HEADS * QK_PAD), BF16),
                   jax.ShapeDtypeStruct((t, MLA_WIDTH), BF16),
                   jax.ShapeDtypeStruct((t, MLA_WIDTH), BF16),
                   jax.ShapeDtypeStruct((t, LANE), BF16)),
        grid=(t // tm,),
        in_specs=[rows(Q_RANK, EV_CQ), rows(KV_RANK, EV_CKV), rows(LANE, EV_KPE),
                  full(qg), full(kvg), full(wq), full(wkv), table(), table(), table()],
        out_specs=(pl.BlockSpec((tm, MLA_HEADS * QK_PAD), lambda i: (i, 0)),
                   pl.BlockSpec((tm, MLA_WIDTH), lambda i: (i, 0)),
                   pl.BlockSpec((tm, MLA_WIDTH), lambda i: (i, 0)),
                   pl.BlockSpec((tm, LANE), lambda i: (i, 0))),
        compiler_params=_params("parallel"),
        name="mla_proj",
    )(p, p, p, qg, kvg, wq, wkv, cos, slo, shi)


def _attn_kernel(q_ref, kn_ref, kp_ref, v_ref, bz_ref, o_ref, kf_ref, vf_ref, m_ref, acc_ref,
                 s_ref, *, tq, heads):
    qi = pl.program_id(2)

    @pl.when(qi == 0)
    def _():
        for h in range(heads):
            kf_ref[h, :, :QK_NOPE] = kn_ref[:, h * QK_NOPE:(h + 1) * QK_NOPE]
            kf_ref[h, :, QK_NOPE:] = kp_ref[...]
            vf_ref[h, :, :V_HEAD] = v_ref[:, h * V_HEAD:(h + 1) * V_HEAD]
            vf_ref[h, :, V_HEAD:] = jnp.ones((vf_ref.shape[1], LANE), BF16)

    m_ref[...] = jnp.full(m_ref.shape, -jnp.inf, F32)
    acc_ref[...] = jnp.zeros(acc_ref.shape, F32)

    def scores(j, slot):
        start = pl.multiple_of(j * tq, tq)
        for h in range(heads):
            q = q_ref[:, h * QK_PAD:(h + 1) * QK_PAD]
            k = kf_ref[h, pl.ds(start, tq), :]
            s_ref[slot, h] = lax.dot_general(q, k, (((1,), (1,)), ((), ())),
                                             preferred_element_type=F32)

    def consume(j, slot, masked):
        start = pl.multiple_of(j * tq, tq)
        for h in range(heads):
            s = s_ref[slot, h]
            if masked:
                row = lax.broadcasted_iota(jnp.int32, s.shape, 0)
                col = lax.broadcasted_iota(jnp.int32, s.shape, 1)
                s = jnp.where(col <= row, s, -jnp.inf)
            m_prev = m_ref[h]
            m_new = jnp.maximum(m_prev, jnp.max(s, axis=-1, keepdims=True))
            alpha = jnp.exp2(m_prev - m_new)
            p = jnp.exp2(s - m_new[:, :1])
            pv = jnp.dot(p.astype(BF16), vf_ref[h, pl.ds(start, tq), :],
                         preferred_element_type=F32)
            acc_ref[h] = acc_ref[h] * jnp.concatenate([alpha, alpha], axis=1) + pv
            m_ref[h] = m_new

    scores(0, 0)
    pairs = qi // 2

    def body(t, carry):
        scores(2 * t + 1, 1)
        consume(2 * t, 0, False)
        scores(2 * t + 2, 0)
        consume(2 * t + 1, 1, False)
        return carry

    lax.fori_loop(0, pairs, body, 0)

    @pl.when(qi % 2 == 0)
    def _():
        consume(qi, 0, True)

    @pl.when(qi % 2 == 1)
    def _():
        scores(qi, 1)
        consume(qi - 1, 0, False)
        consume(qi, 1, True)

    for h in range(heads):
        cs = slice(h * V_HEAD, (h + 1) * V_HEAD)
        o = acc_ref[h, :, :V_HEAD] / acc_ref[h, :, V_HEAD:]
        o_ref[:, cs] = (o * _silu(bz_ref[:, cs])).astype(o_ref.dtype)


def _attention(q, kn, kp, v, p, *, batch, seq, tq=512, heads=2):
    nq = seq // tq
    vw = heads * V_HEAD
    return pl.pallas_call(
        functools.partial(_attn_kernel, tq=tq, heads=heads),
        out_shape=jax.ShapeDtypeStruct((batch * seq, MLA_WIDTH), BF16),
        grid=(batch, MLA_HEADS // heads, nq),
        in_specs=[
            pl.BlockSpec((tq, heads * QK_PAD), lambda b, h, i: (b * nq + i, h)),
            pl.BlockSpec((seq, heads * QK_NOPE), lambda b, h, i: (b, h)),
            pl.BlockSpec((seq, LANE), lambda b, h, i: (b, 0)),
            pl.BlockSpec((seq, vw), lambda b, h, i: (b, h)),
            pl.BlockSpec((tq, vw), lambda b, h, i: (b * nq + i, EV_BZ // vw + h)),
        ],
        out_specs=pl.BlockSpec((tq, vw), lambda b, h, i: (b * nq + i, h)),
        scratch_shapes=[pltpu.VMEM((heads, seq, QK_PAD), BF16),
                        pltpu.VMEM((heads, seq, V_HEAD + LANE), BF16),
                        pltpu.VMEM((heads, tq, LANE), F32),
                        pltpu.VMEM((heads, tq, V_HEAD + LANE), F32),
                        pltpu.VMEM((2, heads, tq, tq), F32)],
        compiler_params=_params("parallel", "parallel", "arbitrary"),
        name="mla_attention",
    )(q, kn, kp, v, p)


def _out_proj_kernel(*refs, n_in, emit_y):
    a_refs = refs[:n_in]
    w_refs = refs[n_in:2 * n_in]
    x_ref, g_ref = refs[2 * n_in:2 * n_in + 2]
    outs = refs[2 * n_in + 2:]
    y = x_ref[...]
    for a_ref, w_ref in zip(a_refs, w_refs):
        y = y + jnp.dot(a_ref[...], w_ref[...], preferred_element_type=F32)
    if emit_y:
        outs[0][...] = y
    h_ref = outs[-1]
    h_ref[...] = _rms(y, g_ref[...]).astype(h_ref.dtype)


def _out_proj(acts, w, x, norm_g, *, last, tm=512, name):
    m, n = x.shape
    n_in = len(acts)
    kw = w.shape[0] // n_in
    row = pl.BlockSpec((tm, n), lambda i: (i, 0))
    in_specs = [pl.BlockSpec((tm, kw), lambda i: (i, 0)) for _ in acts]
    in_specs += [pl.BlockSpec((kw, n), (lambda i, c=c: (c, 0))) for c in range(n_in)]
    in_specs += [row, pl.BlockSpec((1, n), lambda i: (0, 0))]
    args = list(acts) + [w] * n_in + [x, norm_g.reshape(1, n)]
    if last:
        out_shape, out_specs = jax.ShapeDtypeStruct((m, n), F32), row
    else:
        out_shape = (jax.ShapeDtypeStruct((m, n), F32), jax.ShapeDtypeStruct((m, n), BF16))
        out_specs = (row, row)
    return pl.pallas_call(
        functools.partial(_out_proj_kernel, n_in=n_in, emit_y=not last),
        out_shape=out_shape,
        grid=(m // tm,),
        in_specs=in_specs,
        out_specs=out_specs,
        compiler_params=_params("parallel"),
        name=name,
    )(*args)


def _hgrn_consts():
    c = HGRN_CHUNK
    t = np.arange(c)[:, None]
    s = np.arange(c)[None, :]
    masks = []
    for lvl in range(HGRN_LEVELS):
        m = 1 << lvl
        same = (t // (2 * m)) == (s // (2 * m))
        masks.append((same & (t % (2 * m) >= m) & (s % (2 * m) < m)).astype(np.float32))
    return np.stack(masks)


def _hgrn_scan(q, k, f):
    nt = HGRN_CHUNK // SUB
    row = lax.broadcasted_iota(jnp.int32, (SUB, HGRN_K), 0)

    def tiles(x):
        return [x[i * SUB:(i + 1) * SUB, :] for i in range(nt)]

    e = tiles(f)
    qs = [a * b for a, b in zip(tiles(q), e)]
    ks = tiles(k)
    levels = []
    m = 1
    while m < SUB:
        levels.append((qs, ks))
        later = (row & m) != 0
        nq, nk, ne = [], [], []
        for i in range(nt):
            down = pltpu.roll(e[i], m, axis=0)
            up = pltpu.roll(e[i], SUB - m, axis=0)
            nq.append(qs[i] * jnp.where(later, down, 1.0))
            nk.append(ks[i] * jnp.where(later, 1.0, up))
            ne.append(e[i] * jnp.where(later, down, up))
        qs, ks, e = nq, nk, ne
        m *= 2
    mt = 1
    while mt < nt:
        levels.append((qs, ks))
        nq, nk, ne = [], [], []
        for i in range(nt):
            if (i // mt) % 2:
                nq.append(qs[i] * e[i - mt])
                nk.append(ks[i])
            else:
                nq.append(qs[i])
                nk.append(ks[i] * e[i + mt])
            ne.append(e[i] * e[i + mt] if i % (2 * mt) == 0 else ne[(i // (2 * mt)) * 2 * mt])
        qs, ks, e = nq, nk, ne
        mt *= 2
    return levels, (qs, ks), e[0]


def _hgrn_kernel(q_ref, f_ref, v_ref, g_ref, lbl_ref, ng_ref, mask_ref, o_ref, st_ref,
                 *, heads, layer):
    c = HGRN_CHUNK

    @pl.when(pl.program_id(2) == 0)
    def _():
        st_ref[...] = jnp.zeros(st_ref.shape, F32)

    nt = (((1,), (1,)), ((), ()))
    tn = (((0,), (0,)), ((), ()))

    def rows(tiles):
        return jnp.concatenate(tiles, axis=0).astype(BF16)

    for h in range(heads):
        cs = slice(h * HGRN_K, (h + 1) * HGRN_K)
        logits = lbl_ref[:, cs]
        e = jnp.exp(logits - jnp.max(logits, axis=0, keepdims=True))
        sm = e / jnp.sum(e, axis=0, keepdims=True)
        lb = jnp.sum(sm[:layer + 1], axis=0, keepdims=True) - sm[0:1]

        q = q_ref[:, cs]
        v = v_ref[:, cs]
        gate = (1.0 - lb) * _sigmoid(f_ref[:, cs])
        f = lb + gate
        k = (1.0 - lb) - gate
        levels, (q_c, k_c), e_c = _hgrn_scan(q, k, f)

        vb = v.astype(BF16)
        st = st_ref[h]
        a = jnp.zeros((c, c), F32)
        for lvl, (q_l, k_l) in enumerate(levels):
            pair = lax.dot_general(rows(q_l), rows(k_l), nt, preferred_element_type=F32)
            a = a + mask_ref[lvl] * pair
        o = (lax.dot_general(rows(q_c), st.astype(BF16), nt, preferred_element_type=F32)
             + jnp.dot(a.astype(BF16), vb, preferred_element_type=F32)
             + jnp.sum(q * k, axis=-1, keepdims=True) * v)

        st_ref[h] = st * e_c[0:1, :] + lax.dot_general(vb, rows(k_c), tn,
                                                       preferred_element_type=F32)

        ms = jnp.mean(o * o, axis=-1, keepdims=True)
        on = o * lax.rsqrt(ms + EPS) * ng_ref[...]
        o_ref[:, cs] = (on * _silu(g_ref[:, cs])).astype(o_ref.dtype)


def _hgrn(p, lb_logits, norm_g, *, batch, seq, layer, heads=16):
    c = HGRN_CHUNK
    nc = seq // c
    width = heads * HGRN_K
    fdim = HGRN_HEADS * HGRN_K
    nhb = HGRN_HEADS // heads
    masks = jnp.asarray(_hgrn_consts())

    def cols(col0):
        return pl.BlockSpec((c, width), lambda b, h, i: (b * nc + i, col0 // width + h))

    def const(a):
        nd = a.ndim
        return pl.BlockSpec(a.shape, lambda b, h, i: (0,) * nd)

    depth = lb_logits.shape[0]
    ng = norm_g.reshape(1, HGRN_V)
    return pl.pallas_call(
        functools.partial(_hgrn_kernel, heads=heads, layer=layer),
        out_shape=jax.ShapeDtypeStruct((batch * seq, D_MODEL), BF16),
        grid=(batch, nhb, nc),
        in_specs=[cols(0), cols(fdim), cols(2 * fdim), cols(2 * fdim + D_MODEL),
                  pl.BlockSpec((depth, width), lambda b, h, i: (0, h)),
                  const(ng), const(masks)],
        out_specs=pl.BlockSpec((c, width), lambda b, h, i: (b * nc + i, h)),
        scratch_shapes=[pltpu.VMEM((heads, HGRN_V, HGRN_K), F32)],
        compiler_params=_params("parallel", "parallel", "arbitrary"),
        name="hgrn2",
    )(p, p, p, p, lb_logits, ng, masks)


def _rope_tables(seq):
    half = QK_ROPE // 2
    inv_freq = 1.0 / (ROPE_BASE ** (jnp.arange(0, QK_ROPE, 2, dtype=F32) / QK_ROPE))
    ang = jnp.arange(seq, dtype=F32)[:, None] * inv_freq[None, :]
    cos, sin = jnp.cos(ang), jnp.sin(ang)
    z = jnp.zeros((seq, half), F32)
    zz = jnp.zeros((seq, LANE - QK_ROPE), F32)
    cos_t = jnp.concatenate([cos, cos, zz], axis=1)
    sin_lo = jnp.concatenate([-sin, z, zz], axis=1)
    sin_hi = jnp.concatenate([z, sin, zz], axis=1)
    return cos_t, sin_lo, sin_hi


def _even_in_weight_kernel(w_ref, o_ref):
    kpe0 = 3 * CONV_CH + Q_RANK + KV_RANK
    o_ref[:kpe0, :] = w_ref[:kpe0, :].astype(BF16)
    o_ref[EV_BZ:EV_BZ + MLA_WIDTH, :] = w_ref[kpe0 + QK_ROPE:, :].astype(BF16)
    o_ref[EV_KPE:EV_KPE + QK_ROPE, :] = w_ref[kpe0:kpe0 + QK_ROPE, :].astype(BF16)
    o_ref[EV_KPE + QK_ROPE:, :] = jnp.zeros(
        (EV_IN_PAD - EV_KPE - QK_ROPE, o_ref.shape[1]), BF16)


def _even_in_weight(wt, *, tc=512):
    n, d = wt.shape
    return pl.pallas_call(
        _even_in_weight_kernel,
        out_shape=jax.ShapeDtypeStruct((EV_IN_PAD, d), BF16),
        grid=(d // tc,),
        in_specs=[pl.BlockSpec((n, tc), lambda i: (0, i))],
        out_specs=pl.BlockSpec((EV_IN_PAD, tc), lambda i: (0, i)),
        compiler_params=_params("parallel"),
        name="even_in_weight",
    )(wt)


def _even_layer(x, h, norm_g, w_in, conv_w, conv_b, ln_g, ln_b, q_norm_g, w_uq, kv_norm_g, w_ukv,
                w_out, next_g, tables, *, batch, seq, last):
    p = _in_proj(x if h is None else h, norm_g if h is None else None, _even_in_weight(w_in.T),
                 w_transposed=True, tm=1024, tn=768, name="even_in_proj")
    a_out = _conv_branch(p, conv_w, conv_b, ln_g, ln_b, batch=batch, seq=seq)
    wq = jnp.pad(w_uq, ((0, 0), (0, 0), (0, QK_PAD - QK_NOPE - QK_ROPE)))
    wq = wq.reshape(Q_RANK, MLA_HEADS * QK_PAD).astype(BF16)
    wkv = jnp.concatenate([w_ukv[:, :, :QK_NOPE].reshape(KV_RANK, MLA_WIDTH),
                           w_ukv[:, :, QK_NOPE:].reshape(KV_RANK, MLA_WIDTH)], axis=1).astype(BF16)
    q, kn, v, kp = _mla_proj(p, q_norm_g, kv_norm_g, wq, wkv, *tables, seq=seq)
    b_out = _attention(q, kn, kp, v, p, batch=batch, seq=seq)
    return _out_proj([a_out, b_out], w_out.astype(BF16), x, next_g, last=last, name="even_out_proj")


def _odd_layer(x, h, norm_g, w_in, lb_logits, hgrn_norm_g, w_out, next_g, *, batch, seq, layer, last):
    p = _in_proj(x if h is None else h, norm_g if h is None else None, w_in.astype(BF16),
                 tm=1024, tn=1024, name="odd_in_proj")
    o = _hgrn(p, lb_logits, hgrn_norm_g, batch=batch, seq=seq, layer=layer)
    return _out_proj([o], w_out.astype(BF16), x, next_g, last=last, name="odd_out_proj")


def kernel(x, ev_norm_g, ev_w_in, conv_w, conv_b, conv_ln_g, conv_ln_b, mla_q_norm_g, mla_w_uq,
           mla_kv_norm_g, mla_w_ukv, ev_w_out, od_norm_g, od_w_in, hgrn_lb_logits, hgrn_norm_g,
           od_w_out, final_norm_g):
    batch, seq, d = x.shape
    depth = hgrn_lb_logits.shape[0]
    tables = _rope_tables(seq)
    y = x.reshape(batch * seq, d)
    h = None
    for l in range(depth):
        j = l // 2
        last = l == depth - 1
        if last:
            next_g = final_norm_g
        else:
            next_g = od_norm_g[(l + 1) // 2] if l % 2 == 0 else ev_norm_g[(l + 1) // 2]
        if l % 2 == 0:
            out = _even_layer(y, h, ev_norm_g[j], ev_w_in[j], conv_w[j], conv_b[j], conv_ln_g[j],
                              conv_ln_b[j], mla_q_norm_g[j], mla_w_uq[j], mla_kv_norm_g[j],
                              mla_w_ukv[j], ev_w_out[j], next_g, tables, batch=batch, seq=seq,
                              last=last)
        else:
            out = _odd_layer(y, h, od_norm_g[j], od_w_in[j], hgrn_lb_logits, hgrn_norm_g[j],
                             od_w_out[j], next_g, batch=batch, seq=seq, layer=l, last=last)
        if last:
            y = out
        else:
            y, h = out
    return y.reshape(batch, seq, d)
```

```python
import functools
import math

import numpy as np
import jax
import jax.numpy as jnp
from jax import lax
from jax.experimental import pallas as pl
from jax.experimental.pallas import tpu as pltpu

F32 = jnp.float32
BF16 = jnp.bfloat16

D_MODEL = 2048
CONV_CH = 1024
CONV_GROUP = 128
CONV_K = 31
MLA_HEADS = 8
QK_NOPE = 128
QK_ROPE = 64
V_HEAD = 128
Q_RANK = 512
KV_RANK = 512
ROPE_BASE = 10000.0
MLA_WIDTH = MLA_HEADS * V_HEAD
QK_PAD = 256
HGRN_HEADS = 16
HGRN_K = 128
HGRN_V = 128
EPS = 1e-6

LANE = 128
SUB = 8
HALO = 32
VMEM_LIMIT = 56 * 1024 * 1024

HGRN_CHUNK = 128
HGRN_LEVELS = 7


def _params(*sem):
    return pltpu.CompilerParams(dimension_semantics=sem, vmem_limit_bytes=VMEM_LIMIT)


def _sigmoid(x):
    return 1.0 / (1.0 + jnp.exp(-x))


def _silu(x):
    return x * _sigmoid(x)


def _rms(x, g):
    ms = jnp.mean(x * x, axis=-1, keepdims=True)
    return x * lax.rsqrt(ms + EPS) * g


def _norm_matmul_kernel(x_ref, g_ref, w_ref, o_ref, h_ref):
    @pl.when(pl.program_id(1) == 0)
    def _():
        h_ref[...] = _rms(x_ref[...], g_ref[...]).astype(BF16)

    o_ref[...] = jnp.dot(h_ref[...], w_ref[...], preferred_element_type=F32)


def _matmul_kernel(h_ref, w_ref, o_ref):
    o_ref[...] = jnp.dot(h_ref[...], w_ref[...], preferred_element_type=F32)


def _in_proj(x, g, w, *, tm, tn, name):
    m, d = x.shape
    n = w.shape[1]
    in_specs = [pl.BlockSpec((tm, d), lambda i, j: (i, 0)),
                pl.BlockSpec((d, tn), lambda i, j: (0, j))]
    args = [x, w]
    scratch = []
    body = _matmul_kernel
    if g is not None:
        in_specs.insert(1, pl.BlockSpec((1, d), lambda i, j: (0, 0)))
        args.insert(1, g.reshape(1, d))
        scratch = [pltpu.VMEM((tm, d), BF16)]
        body = _norm_matmul_kernel
    return pl.pallas_call(
        body,
        out_shape=jax.ShapeDtypeStruct((m, n), F32),
        grid=(m // tm, n // tn),
        in_specs=in_specs,
        out_specs=pl.BlockSpec((tm, tn), lambda i, j: (i, j)),
        scratch_shapes=scratch,
        compiler_params=_params("parallel", "arbitrary"),
        name=name,
    )(*args)


EV_TILE = 3 * 256
EV_CONV_TILES = CONV_CH // 256
EV_MLA_COLS = 3 * EV_TILE
MLA_CQ, MLA_CKV, MLA_BZ, MLA_KPE = 0, 512, 1024, 2048


def _even_front_kernel(x_ref, g_ref, w_ref, cw_ref, cb_ref, lg_ref, lb_ref, a_ref, p_ref,
                       h_ref, mm_ref, u_ref, halo_ref, *, tm, rows, blocks_per_seq):
    i = pl.program_id(0)
    j = pl.program_id(1)
    tc = EV_TILE // 3

    def project():
        return lax.dot_general(h_ref[...], w_ref[...], (((1,), (1,)), ((), ())),
                               preferred_element_type=F32)

    def conv(slot, c):
        u = mm_ref[slot, :, 0:tc] * _sigmoid(mm_ref[slot, :, tc:2 * tc])
        u_ref[HALO:, :] = u
        u_ref[:HALO, :] = jnp.where(i % blocks_per_seq == 0, 0.0, halo_ref[c])
        halo_ref[c] = u[tm - HALO:, :]
        w = cw_ref[...]
        off = HALO - (CONV_K - 1)
        for r0 in range(0, tm, rows):
            y = cb_ref[...]
            for res in range(SUB):
                taps = [k for k in range(CONV_K) if (off + k) % SUB == res]
                n = rows + (SUB if res else 0)
                z = None
                for k in taps:
                    term = w[k:k + 1, :] * u_ref[pl.ds(r0 + off + k - res, n), :]
                    z = term if z is None else z + term
                y = y + z[res:res + rows, :]
            outs = []
            for c0 in range(0, tc, CONV_GROUP):
                yg = y[:, c0:c0 + CONV_GROUP]
                mu = jnp.mean(yg, axis=-1, keepdims=True)
                d = yg - mu
                var = jnp.mean(d * d, axis=-1, keepdims=True)
                outs.append(d * lax.rsqrt(var + EPS))
            yn = jnp.concatenate(outs, axis=-1) * lg_ref[...] + lb_ref[...]
            z = mm_ref[slot, pl.ds(r0, rows), 2 * tc:3 * tc]
            a_ref[pl.ds(r0, rows), :] = (_silu(yn) * _silu(z)).astype(a_ref.dtype)

    @pl.when((i == 0) & (j == 0))
    def _():
        halo_ref[...] = jnp.zeros(halo_ref.shape, F32)

    @pl.when(j == 0)
    def _():
        if x_ref.dtype == BF16:
            h_ref[...] = x_ref[...]
        else:
            h_ref[...] = _rms(x_ref[...], g_ref[...]).astype(BF16)
        mm_ref[0] = project()

    for t in range(1, EV_CONV_TILES):
        @pl.when(j == t)
        def _(t=t):
            mm_ref[t % 2] = project()
            conv((t - 1) % 2, t - 1)

    @pl.when(j == EV_CONV_TILES)
    def _():
        p_ref[...] = project().astype(p_ref.dtype)
        conv((EV_CONV_TILES - 1) % 2, EV_CONV_TILES - 1)

    @pl.when(j > EV_CONV_TILES)
    def _():
        p_ref[...] = project().astype(p_ref.dtype)


def _even_front(x, g, wt, conv_w, conv_b, ln_g, ln_b, *, seq, tm=1024, rows=32):
    m, d = x.shape
    tc = EV_TILE // 3
    n_steps = wt.shape[0] // EV_TILE
    last_conv = EV_CONV_TILES - 1

    def conv_tile(j):
        return jnp.clip(j - 1, 0, last_conv)

    def vec():
        return pl.BlockSpec((1, tc), lambda i, j: (0, conv_tile(j)))

    return pl.pallas_call(
        functools.partial(_even_front_kernel, tm=tm, rows=rows, blocks_per_seq=seq // tm),
        out_shape=(jax.ShapeDtypeStruct((m, CONV_CH), BF16),
                   jax.ShapeDtypeStruct((m, EV_MLA_COLS), BF16)),
        grid=(m // tm, n_steps),
        in_specs=[pl.BlockSpec((tm, d), lambda i, j: (i, 0)),
                  pl.BlockSpec((1, d), lambda i, j: (0, 0)),
                  pl.BlockSpec((EV_TILE, d), lambda i, j: (j, 0)),
                  pl.BlockSpec((CONV_K, tc), lambda i, j: (0, conv_tile(j))),
                  vec(), vec(), vec()],
        out_specs=(pl.BlockSpec((tm, tc), lambda i, j: (i, conv_tile(j))),
                   pl.BlockSpec((tm, EV_TILE), lambda i, j: (i, jnp.maximum(j - EV_CONV_TILES, 0)))),
        scratch_shapes=[pltpu.VMEM((tm, d), BF16),
                        pltpu.VMEM((2, tm, EV_TILE), F32),
                        pltpu.VMEM((HALO + tm, tc), F32),
                        pltpu.VMEM((EV_CONV_TILES, HALO, tc), F32)],
        compiler_params=_params("arbitrary", "arbitrary"),
        name="even_front",
    )(x, g.reshape(1, d), wt, conv_w, conv_b.reshape(1, -1), ln_g.reshape(1, -1), ln_b.reshape(1, -1))


def _rope128(x, cos, sin_lo, sin_hi):
    half = QK_ROPE // 2
    return (x * cos + pltpu.roll(x, LANE - half, axis=1) * sin_lo
            + pltpu.roll(x, half, axis=1) * sin_hi)


def _mla_proj_kernel(cq_ref, ckv_ref, kpe_ref, qg_ref, kvg_ref, wq_ref, wkv_ref,
                     cos_ref, slo_ref, shi_ref, q_ref, kn_ref, v_ref, kp_ref, *, scale):
    cos, slo, shi = cos_ref[...], slo_ref[...], shi_ref[...]
    hq = _rms(cq_ref[...].astype(F32), qg_ref[...]).astype(BF16)
    q = jnp.dot(hq, wq_ref[...], preferred_element_type=F32) * scale
    for h in range(MLA_HEADS):
        c0 = h * QK_PAD
        q_ref[:, c0:c0 + QK_NOPE] = q[:, c0:c0 + QK_NOPE].astype(BF16)
        pe = q[:, c0 + QK_NOPE:c0 + QK_PAD]
        q_ref[:, c0 + QK_NOPE:c0 + QK_PAD] = _rope128(pe, cos, slo, shi).astype(BF16)
    hkv = _rms(ckv_ref[...].astype(F32), kvg_ref[...]).astype(BF16)
    kv = jnp.dot(hkv, wkv_ref[...], preferred_element_type=F32)
    kn_ref[...] = kv[:, :MLA_WIDTH].astype(BF16)
    v_ref[...] = kv[:, MLA_WIDTH:].astype(BF16)
    kp_ref[...] = _rope128(kpe_ref[...].astype(F32), cos, slo, shi).astype(BF16)


def _mla_proj(p, q_norm_g, kv_norm_g, wq, wkv, cos, slo, shi, *, seq, tm=512):
    t = p.shape[0]
    nsb = seq // tm
    scale = float((QK_NOPE + QK_ROPE) ** -0.5 * math.log2(math.e))

    def rows(width, col0):
        return pl.BlockSpec((tm, width), lambda i: (i, col0 // width))

    def full(a):
        return pl.BlockSpec(a.shape, lambda i: (0, 0))

    def table():
        return pl.BlockSpec((tm, LANE), lambda i: (i % nsb, 0))

    qg = q_norm_g.reshape(1, -1)
    kvg = kv_norm_g.reshape(1, -1)
    return pl.pallas_call(
        functools.partial(_mla_proj_kernel, scale=scale),
        out_shape=(jax.ShapeDtypeStruct((t, MLA_HEADS * QK_PAD), BF16),
                   jax.ShapeDtypeStruct((t, MLA_WIDTH), BF16),
                   jax.ShapeDtypeStruct((t, MLA_WIDTH), BF16),
                   jax.ShapeDtypeStruct((t, LANE), BF16)),
        grid=(t // tm,),
        in_specs=[rows(Q_RANK, MLA_CQ), rows(KV_RANK, MLA_CKV), rows(LANE, MLA_KPE),
                  full(qg), full(kvg), full(wq), full(wkv), table(), table(), table()],
        out_specs=(pl.BlockSpec((tm, MLA_HEADS * QK_PAD), lambda i: (i, 0)),
                   pl.BlockSpec((tm, MLA_WIDTH), lambda i: (i, 0)),
                   pl.BlockSpec((tm, MLA_WIDTH), lambda i: (i, 0)),
                   pl.BlockSpec((tm, LANE), lambda i: (i, 0))),
        compiler_params=_params("parallel"),
        name="mla_proj",
    )(p, p, p, qg, kvg, wq, wkv, cos, slo, shi)


def _attn_kernel(q_ref, kn_ref, kp_ref, v_ref, bz_ref, o_ref, kf_ref, vf_ref, m_ref, acc_ref,
                 s_ref, *, tq, heads):
    qi = pl.program_id(2)

    @pl.when(qi == 0)
    def _():
        for h in range(heads):
            kf_ref[h, :, :QK_NOPE] = kn_ref[:, h * QK_NOPE:(h + 1) * QK_NOPE]
            kf_ref[h, :, QK_NOPE:] = kp_ref[...]
            vf_ref[h, :, :V_HEAD] = v_ref[:, h * V_HEAD:(h + 1) * V_HEAD]
            vf_ref[h, :, V_HEAD:] = jnp.ones((vf_ref.shape[1], LANE), BF16)

    m_ref[...] = jnp.full(m_ref.shape, -jnp.inf, F32)
    acc_ref[...] = jnp.zeros(acc_ref.shape, F32)

    def scores(j, slot):
        start = pl.multiple_of(j * tq, tq)
        for h in range(heads):
            q = q_ref[:, h * QK_PAD:(h + 1) * QK_PAD]
            k = kf_ref[h, pl.ds(start, tq), :]
            s_ref[slot, h] = lax.dot_general(q, k, (((1,), (1,)), ((), ())),
                                             preferred_element_type=F32)

    def consume(j, slot, masked):
        start = pl.multiple_of(j * tq, tq)
        for h in range(heads):
            s = s_ref[slot, h]
            if masked:
                row = lax.broadcasted_iota(jnp.int32, s.shape, 0)
                col = lax.broadcasted_iota(jnp.int32, s.shape, 1)
                s = jnp.where(col <= row, s, -jnp.inf)
            m_prev = m_ref[h]
            m_new = jnp.maximum(m_prev, jnp.max(s, axis=-1, keepdims=True))
            alpha = jnp.exp2(m_prev - m_new)
            p = jnp.exp2(s - m_new[:, :1])
            pv = jnp.dot(p.astype(BF16), vf_ref[h, pl.ds(start, tq), :],
                         preferred_element_type=F32)
            acc_ref[h] = acc_ref[h] * jnp.concatenate([alpha, alpha], axis=1) + pv
            m_ref[h] = m_new

    scores(0, 0)
    pairs = qi // 2

    def body(t, carry):
        scores(2 * t + 1, 1)
        consume(2 * t, 0, False)
        scores(2 * t + 2, 0)
        consume(2 * t + 1, 1, False)
        return carry

    lax.fori_loop(0, pairs, body, 0)

    @pl.when(qi % 2 == 0)
    def _():
        consume(qi, 0, True)

    @pl.when(qi % 2 == 1)
    def _():
        scores(qi, 1)
        consume(qi - 1, 0, False)
        consume(qi, 1, True)

    for h in range(heads):
        cs = slice(h * V_HEAD, (h + 1) * V_HEAD)
        o = acc_ref[h, :, :V_HEAD] / acc_ref[h, :, V_HEAD:]
        o_ref[:, cs] = (o * _silu(bz_ref[:, cs].astype(F32))).astype(o_ref.dtype)


def _attention(q, kn, kp, v, p, *, batch, seq, tq=512, heads=2):
    nq = seq // tq
    vw = heads * V_HEAD
    return pl.pallas_call(
        functools.partial(_attn_kernel, tq=tq, heads=heads),
        out_shape=jax.ShapeDtypeStruct((batch * seq, MLA_WIDTH), BF16),
        grid=(batch, MLA_HEADS // heads, nq),
        in_specs=[
            pl.BlockSpec((tq, heads * QK_PAD), lambda b, h, i: (b * nq + i, h)),
            pl.BlockSpec((seq, heads * QK_NOPE), lambda b, h, i: (b, h)),
            pl.BlockSpec((seq, LANE), lambda b, h, i: (b, 0)),
            pl.BlockSpec((seq, vw), lambda b, h, i: (b, h)),
            pl.BlockSpec((tq, vw), lambda b, h, i: (b * nq + i, MLA_BZ // vw + h)),
        ],
        out_specs=pl.BlockSpec((tq, vw), lambda b, h, i: (b * nq + i, h)),
        scratch_shapes=[pltpu.VMEM((heads, seq, QK_PAD), BF16),
                        pltpu.VMEM((heads, seq, V_HEAD + LANE), BF16),
                        pltpu.VMEM((heads, tq, LANE), F32),
                        pltpu.VMEM((heads, tq, V_HEAD + LANE), F32),
                        pltpu.VMEM((2, heads, tq, tq), F32)],
        compiler_params=_params("parallel", "parallel", "arbitrary"),
        name="mla_attention",
    )(q, kn, kp, v, p)


def _out_proj_kernel(*refs, n_in, emit_y):
    a_refs = refs[:n_in]
    w_refs = refs[n_in:2 * n_in]
    x_ref, g_ref = refs[2 * n_in:2 * n_in + 2]
    outs = refs[2 * n_in + 2:]
    y = x_ref[...]
    for a_ref, w_ref in zip(a_refs, w_refs):
        y = y + jnp.dot(a_ref[...], w_ref[...], preferred_element_type=F32)
    if emit_y:
        outs[0][...] = y
    h_ref = outs[-1]
    h_ref[...] = _rms(y, g_ref[...]).astype(h_ref.dtype)


def _out_proj(acts, w, x, norm_g, *, last, tm=512, name):
    m, n = x.shape
    n_in = len(acts)
    kw = w.shape[0] // n_in
    row = pl.BlockSpec((tm, n), lambda i: (i, 0))
    in_specs = [pl.BlockSpec((tm, kw), lambda i: (i, 0)) for _ in acts]
    in_specs += [pl.BlockSpec((kw, n), (lambda i, c=c: (c, 0))) for c in range(n_in)]
    in_specs += [row, pl.BlockSpec((1, n), lambda i: (0, 0))]
    args = list(acts) + [w] * n_in + [x, norm_g.reshape(1, n)]
    if last:
        out_shape, out_specs = jax.ShapeDtypeStruct((m, n), F32), row
    else:
        out_shape = (jax.ShapeDtypeStruct((m, n), F32), jax.ShapeDtypeStruct((m, n), BF16))
        out_specs = (row, row)
    return pl.pallas_call(
        functools.partial(_out_proj_kernel, n_in=n_in, emit_y=not last),
        out_shape=out_shape,
        grid=(m // tm,),
        in_specs=in_specs,
        out_specs=out_specs,
        compiler_params=_params("parallel"),
        name=name,
    )(*args)


def _hgrn_consts():
    c = HGRN_CHUNK
    t = np.arange(c)[:, None]
    s = np.arange(c)[None, :]
    masks = []
    for lvl in range(HGRN_LEVELS):
        m = 1 << lvl
        same = (t // (2 * m)) == (s // (2 * m))
        masks.append((same & (t % (2 * m) >= m) & (s % (2 * m) < m)).astype(np.float32))
    return np.stack(masks)


def _hgrn_scan(q, k, f):
    nt = HGRN_CHUNK // SUB
    row = lax.broadcasted_iota(jnp.int32, (SUB, HGRN_K), 0)

    def tiles(x):
        return [x[i * SUB:(i + 1) * SUB, :] for i in range(nt)]

    e = tiles(f)
    qs = [a * b for a, b in zip(tiles(q), e)]
    ks = tiles(k)
    levels = []
    m = 1
    while m < SUB:
        levels.append((qs, ks))
        later = (row & m) != 0
        nq, nk, ne = [], [], []
        for i in range(nt):
            down = pltpu.roll(e[i], m, axis=0)
            up = pltpu.roll(e[i], SUB - m, axis=0)
            nq.append(qs[i] * jnp.where(later, down, 1.0))
            nk.append(ks[i] * jnp.where(later, 1.0, up))
            ne.append(e[i] * jnp.where(later, down, up))
        qs, ks, e = nq, nk, ne
        m *= 2
    mt = 1
    while mt < nt:
        levels.append((qs, ks))
        nq, nk, ne = [], [], []
        for i in range(nt):
            if (i // mt) % 2:
                nq.append(qs[i] * e[i - mt])
                nk.append(ks[i])
            else:
                nq.append(qs[i])
                nk.append(ks[i] * e[i + mt])
            ne.append(e[i] * e[i + mt] if i % (2 * mt) == 0 else ne[(i // (2 * mt)) * 2 * mt])
        qs, ks, e = nq, nk, ne
        mt *= 2
    return levels, (qs, ks), e[0]


def _hgrn_kernel(q_ref, f_ref, v_ref, g_ref, lbl_ref, ng_ref, mask_ref, o_ref, st_ref,
                 *, heads, layer):
    c = HGRN_CHUNK

    @pl.when(pl.program_id(2) == 0)
    def _():
        st_ref[...] = jnp.zeros(st_ref.shape, F32)

    nt = (((1,), (1,)), ((), ()))
    tn = (((0,), (0,)), ((), ()))

    def rows(tiles):
        return jnp.concatenate(tiles, axis=0).astype(BF16)

    for h in range(heads):
        cs = slice(h * HGRN_K, (h + 1) * HGRN_K)
        logits = lbl_ref[:, cs]
        e = jnp.exp(logits - jnp.max(logits, axis=0, keepdims=True))
        sm = e / jnp.sum(e, axis=0, keepdims=True)
        lb = jnp.sum(sm[:layer + 1], axis=0, keepdims=True) - sm[0:1]

        q = q_ref[:, cs]
        v = v_ref[:, cs]
        gate = (1.0 - lb) * _sigmoid(f_ref[:, cs])
        f = lb + gate
        k = (1.0 - lb) - gate
        levels, (q_c, k_c), e_c = _hgrn_scan(q, k, f)

        vb = v.astype(BF16)
        st = st_ref[h]
        a = jnp.zeros((c, c), F32)
        for lvl, (q_l, k_l) in enumerate(levels):
            pair = lax.dot_general(rows(q_l), rows(k_l), nt, preferred_element_type=F32)
            a = a + mask_ref[lvl] * pair
        o = (lax.dot_general(rows(q_c), st.astype(BF16), nt, preferred_element_type=F32)
             + jnp.dot(a.astype(BF16), vb, preferred_element_type=F32)
             + jnp.sum(q * k, axis=-1, keepdims=True) * v)

        st_ref[h] = st * e_c[0:1, :] + lax.dot_general(vb, rows(k_c), tn,
                                                       preferred_element_type=F32)

        ms = jnp.mean(o * o, axis=-1, keepdims=True)
        on = o * lax.rsqrt(ms + EPS) * ng_ref[...]
        o_ref[:, cs] = (on * _silu(g_ref[:, cs])).astype(o_ref.dtype)


def _hgrn(p, lb_logits, norm_g, *, batch, seq, layer, heads=16):
    c = HGRN_CHUNK
    nc = seq // c
    width = heads * HGRN_K
    fdim = HGRN_HEADS * HGRN_K
    nhb = HGRN_HEADS // heads
    masks = jnp.asarray(_hgrn_consts())

    def cols(col0):
        return pl.BlockSpec((c, width), lambda b, h, i: (b * nc + i, col0 // width + h))

    def const(a):
        nd = a.ndim
        return pl.BlockSpec(a.shape, lambda b, h, i: (0,) * nd)

    depth = lb_logits.shape[0]
    ng = norm_g.reshape(1, HGRN_V)
    return pl.pallas_call(
        functools.partial(_hgrn_kernel, heads=heads, layer=layer),
        out_shape=jax.ShapeDtypeStruct((batch * seq, D_MODEL), BF16),
        grid=(batch, nhb, nc),
        in_specs=[cols(0), cols(fdim), cols(2 * fdim), cols(2 * fdim + D_MODEL),
                  pl.BlockSpec((depth, width), lambda b, h, i: (0, h)),
                  const(ng), const(masks)],
        out_specs=pl.BlockSpec((c, width), lambda b, h, i: (b * nc + i, h)),
        scratch_shapes=[pltpu.VMEM((heads, HGRN_V, HGRN_K), F32)],
        compiler_params=_params("parallel", "parallel", "arbitrary"),
        name="hgrn2",
    )(p, p, p, p, lb_logits, ng, masks)


def _rope_tables(seq):
    half = QK_ROPE // 2
    inv_freq = 1.0 / (ROPE_BASE ** (jnp.arange(0, QK_ROPE, 2, dtype=F32) / QK_ROPE))
    ang = jnp.arange(seq, dtype=F32)[:, None] * inv_freq[None, :]
    cos, sin = jnp.cos(ang), jnp.sin(ang)
    z = jnp.zeros((seq, half), F32)
    zz = jnp.zeros((seq, LANE - QK_ROPE), F32)
    cos_t = jnp.concatenate([cos, cos, zz], axis=1)
    sin_lo = jnp.concatenate([-sin, z, zz], axis=1)
    sin_hi = jnp.concatenate([z, sin, zz], axis=1)
    return cos_t, sin_lo, sin_hi


def _even_in_weight_kernel(w_ref, o_ref):
    tc = EV_TILE // 3
    for t in range(EV_CONV_TILES):
        for part in range(3):
            src = part * CONV_CH + t * tc
            dst = t * EV_TILE + part * tc
            o_ref[dst:dst + tc, :] = w_ref[src:src + tc, :].astype(BF16)
    mla0 = EV_CONV_TILES * EV_TILE
    cq0 = 3 * CONV_CH
    kpe0 = cq0 + Q_RANK + KV_RANK
    o_ref[mla0:mla0 + MLA_BZ, :] = w_ref[cq0:kpe0, :].astype(BF16)
    o_ref[mla0 + MLA_BZ:mla0 + MLA_KPE, :] = w_ref[kpe0 + QK_ROPE:, :].astype(BF16)
    o_ref[mla0 + MLA_KPE:mla0 + MLA_KPE + QK_ROPE, :] = w_ref[kpe0:kpe0 + QK_ROPE, :].astype(BF16)
    o_ref[mla0 + MLA_KPE + QK_ROPE:, :] = jnp.zeros(
        (EV_MLA_COLS - MLA_KPE - QK_ROPE, o_ref.shape[1]), BF16)


def _even_in_weight(wt, *, tc=512):
    n, d = wt.shape
    n_out = EV_CONV_TILES * EV_TILE + EV_MLA_COLS
    return pl.pallas_call(
        _even_in_weight_kernel,
        out_shape=jax.ShapeDtypeStruct((n_out, d), BF16),
        grid=(d // tc,),
        in_specs=[pl.BlockSpec((n, tc), lambda i: (0, i))],
        out_specs=pl.BlockSpec((n_out, tc), lambda i: (0, i)),
        compiler_params=_params("parallel"),
        name="even_in_weight",
    )(wt)


def _even_layer(x, h, norm_g, w_in, conv_w, conv_b, ln_g, ln_b, q_norm_g, w_uq, kv_norm_g, w_ukv,
                w_out, next_g, tables, *, batch, seq, last):
    a_out, p = _even_front(x if h is None else h, norm_g, _even_in_weight(w_in.T), conv_w, conv_b,
                           ln_g, ln_b, seq=seq)
    wq = jnp.pad(w_uq, ((0, 0), (0, 0), (0, QK_PAD - QK_NOPE - QK_ROPE)))
    wq = wq.reshape(Q_RANK, MLA_HEADS * QK_PAD).astype(BF16)
    wkv = jnp.concatenate([w_ukv[:, :, :QK_NOPE].reshape(KV_RANK, MLA_WIDTH),
                           w_ukv[:, :, QK_NOPE:].reshape(KV_RANK, MLA_WIDTH)], axis=1).astype(BF16)
    q, kn, v, kp = _mla_proj(p, q_norm_g, kv_norm_g, wq, wkv, *tables, seq=seq)
    b_out = _attention(q, kn, kp, v, p, batch=batch, seq=seq)
    return _out_proj([a_out, b_out], w_out.astype(BF16), x, next_g, last=last, name="even_out_proj")


def _odd_layer(x, h, norm_g, w_in, lb_logits, hgrn_norm_g, w_out, next_g, *, batch, seq, layer, last):
    p = _in_proj(x if h is None else h, norm_g if h is None else None, w_in.astype(BF16),
                 tm=1024, tn=1024, name="odd_in_proj")
    o = _hgrn(p, lb_logits, hgrn_norm_g, batch=batch, seq=seq, layer=layer)
    return _out_proj([o], w_out.astype(BF16), x, next_g, last=last, name="odd_out_proj")


def kernel(x, ev_norm_g, ev_w_in, conv_w, conv_b, conv_ln_g, conv_ln_b, mla_q_norm_g, mla_w_uq,
           mla_kv_norm_g, mla_w_ukv, ev_w_out, od_norm_g, od_w_in, hgrn_lb_logits, hgrn_norm_g,
           od_w_out, final_norm_g):
    batch, seq, d = x.shape
    depth = hgrn_lb_logits.shape[0]
    tables = _rope_tables(seq)
    y = x.reshape(batch * seq, d)
    h = None
    for l in range(depth):
        j = l // 2
        last = l == depth - 1
        if last:
            next_g = final_norm_g
        else:
            next_g = od_norm_g[(l + 1) // 2] if l % 2 == 0 else ev_norm_g[(l + 1) // 2]
        if l % 2 == 0:
            out = _even_layer(y, h, ev_norm_g[j], ev_w_in[j], conv_w[j], conv_b[j], conv_ln_g[j],
                              conv_ln_b[j], mla_q_norm_g[j], mla_w_uq[j], mla_kv_norm_g[j],
                              mla_w_ukv[j], ev_w_out[j], next_g, tables, batch=batch, seq=seq,
                              last=last)
        else:
            out = _odd_layer(y, h, od_norm_g[j], od_w_in[j], hgrn_lb_logits, hgrn_norm_g[j],
                             od_w_out[j], next_g, batch=batch, seq=seq, layer=l, last=last)
        if last:
            y = out
        else:
            y, h = out
    return y.reshape(batch, seq, d)
```

```python
import functools
import math

import numpy as np
import jax
import jax.numpy as jnp
from jax import lax
from jax.experimental import pallas as pl
from jax.experimental.pallas import tpu as pltpu

F32 = jnp.float32
BF16 = jnp.bfloat16

D_MODEL = 2048
CONV_CH = 1024
CONV_GROUP = 128
CONV_K = 31
MLA_HEADS = 8
QK_NOPE = 128
QK_ROPE = 64
V_HEAD = 128
Q_RANK = 512
KV_RANK = 512
ROPE_BASE = 10000.0
MLA_WIDTH = MLA_HEADS * V_HEAD
QK_PAD = 256
HGRN_HEADS = 16
HGRN_K = 128
HGRN_V = 128
EPS = 1e-6

LANE = 128
SUB = 8
HALO = 32
VMEM_LIMIT = 56 * 1024 * 1024

HGRN_CHUNK = 128
HGRN_LEVELS = 7


def _params(*sem):
    return pltpu.CompilerParams(dimension_semantics=sem, vmem_limit_bytes=VMEM_LIMIT)


def _sigmoid(x):
    return 1.0 / (1.0 + jnp.exp(-x))


def _silu(x):
    return x * _sigmoid(x)


def _rms(x, g):
    ms = jnp.mean(x * x, axis=-1, keepdims=True)
    return x * lax.rsqrt(ms + EPS) * g


def _rms_rows_kernel(x_ref, g_ref, o_ref):
    o_ref[...] = _rms(x_ref[...], g_ref[...]).astype(o_ref.dtype)


def _rms_rows(x, g, *, tm=512):
    m, d = x.shape
    return pl.pallas_call(
        _rms_rows_kernel,
        out_shape=jax.ShapeDtypeStruct((m, d), BF16),
        grid=(m // tm,),
        in_specs=[pl.BlockSpec((tm, d), lambda i: (i, 0)), pl.BlockSpec((1, d), lambda i: (0, 0))],
        out_specs=pl.BlockSpec((tm, d), lambda i: (i, 0)),
        compiler_params=_params("parallel"),
        name="rms_rows",
    )(x, g.reshape(1, d))


def _in_proj_kernel(h_ref, w_ref, o_ref, wb_ref):
    @pl.when(pl.program_id(1) == 0)
    def _():
        wb_ref[...] = w_ref[...].astype(BF16)

    o_ref[...] = jnp.dot(h_ref[...], wb_ref[...], preferred_element_type=F32)


def _in_proj(h, w, *, tm, tn, name):
    m, d = h.shape
    n = w.shape[1]
    return pl.pallas_call(
        _in_proj_kernel,
        out_shape=jax.ShapeDtypeStruct((m, n), F32),
        grid=(n // tn, m // tm),
        in_specs=[pl.BlockSpec((tm, d), lambda j, i: (i, 0)),
                  pl.BlockSpec((d, tn), lambda j, i: (0, j))],
        out_specs=pl.BlockSpec((tm, tn), lambda j, i: (i, j)),
        scratch_shapes=[pltpu.VMEM((d, tn), BF16)],
        compiler_params=_params("parallel", "arbitrary"),
        name=name,
    )(h, w)


EV_TILE = 3 * 256
EV_CONV_TILES = CONV_CH // 256
EV_MLA_COLS = 3 * EV_TILE
MLA_CQ, MLA_CKV, MLA_BZ, MLA_KPE = 0, 512, 1024, 2048


def _even_front_kernel(x_ref, g_ref, w_ref, cw_ref, cb_ref, lg_ref, lb_ref, a_ref, p_ref,
                       h_ref, mm_ref, u_ref, halo_ref, *, tm, rows, blocks_per_seq):
    i = pl.program_id(0)
    j = pl.program_id(1)
    tc = EV_TILE // 3

    def project():
        return lax.dot_general(h_ref[...], w_ref[...], (((1,), (1,)), ((), ())),
                               preferred_element_type=F32)

    def conv(slot, c):
        u = mm_ref[slot, :, 0:tc] * _sigmoid(mm_ref[slot, :, tc:2 * tc])
        u_ref[HALO:, :] = u
        u_ref[:HALO, :] = jnp.where(i % blocks_per_seq == 0, 0.0, halo_ref[c])
        halo_ref[c] = u[tm - HALO:, :]
        w = cw_ref[...]
        off = HALO - (CONV_K - 1)
        for r0 in range(0, tm, rows):
            y = cb_ref[...]
            for res in range(SUB):
                taps = [k for k in range(CONV_K) if (off + k) % SUB == res]
                n = rows + (SUB if res else 0)
                z = None
                for k in taps:
                    term = w[k:k + 1, :] * u_ref[pl.ds(r0 + off + k - res, n), :]
                    z = term if z is None else z + term
                y = y + z[res:res + rows, :]
            outs = []
            for c0 in range(0, tc, CONV_GROUP):
                yg = y[:, c0:c0 + CONV_GROUP]
                mu = jnp.mean(yg, axis=-1, keepdims=True)
                d = yg - mu
                var = jnp.mean(d * d, axis=-1, keepdims=True)
                outs.append(d * lax.rsqrt(var + EPS))
            yn = jnp.concatenate(outs, axis=-1) * lg_ref[...] + lb_ref[...]
            z = mm_ref[slot, pl.ds(r0, rows), 2 * tc:3 * tc]
            a_ref[pl.ds(r0, rows), :] = (_silu(yn) * _silu(z)).astype(a_ref.dtype)

    @pl.when((i == 0) & (j == 0))
    def _():
        halo_ref[...] = jnp.zeros(halo_ref.shape, F32)

    @pl.when(j == 0)
    def _():
        if x_ref.dtype == BF16:
            h_ref[...] = x_ref[...]
        else:
            h_ref[...] = _rms(x_ref[...], g_ref[...]).astype(BF16)
        mm_ref[0] = project()

    for t in range(1, EV_CONV_TILES):
        @pl.when(j == t)
        def _(t=t):
            mm_ref[t % 2] = project()
            conv((t - 1) % 2, t - 1)

    @pl.when(j == EV_CONV_TILES)
    def _():
        p_ref[...] = project().astype(p_ref.dtype)
        conv((EV_CONV_TILES - 1) % 2, EV_CONV_TILES - 1)

    @pl.when(j > EV_CONV_TILES)
    def _():
        p_ref[...] = project().astype(p_ref.dtype)


def _even_front(x, g, wt, conv_w, conv_b, ln_g, ln_b, *, seq, tm=1024, rows=32):
    m, d = x.shape
    tc = EV_TILE // 3
    n_steps = wt.shape[0] // EV_TILE
    last_conv = EV_CONV_TILES - 1

    def conv_tile(j):
        return jnp.clip(j - 1, 0, last_conv)

    def vec():
        return pl.BlockSpec((1, tc), lambda i, j: (0, conv_tile(j)))

    return pl.pallas_call(
        functools.partial(_even_front_kernel, tm=tm, rows=rows, blocks_per_seq=seq // tm),
        out_shape=(jax.ShapeDtypeStruct((m, CONV_CH), BF16),
                   jax.ShapeDtypeStruct((m, EV_MLA_COLS), BF16)),
        grid=(m // tm, n_steps),
        in_specs=[pl.BlockSpec((tm, d), lambda i, j: (i, 0)),
                  pl.BlockSpec((1, d), lambda i, j: (0, 0)),
                  pl.BlockSpec((EV_TILE, d), lambda i, j: (j, 0)),
                  pl.BlockSpec((CONV_K, tc), lambda i, j: (0, conv_tile(j))),
                  vec(), vec(), vec()],
        out_specs=(pl.BlockSpec((tm, tc), lambda i, j: (i, conv_tile(j))),
                   pl.BlockSpec((tm, EV_TILE), lambda i, j: (i, jnp.maximum(j - EV_CONV_TILES, 0)))),
        scratch_shapes=[pltpu.VMEM((tm, d), BF16),
                        pltpu.VMEM((2, tm, EV_TILE), F32),
                        pltpu.VMEM((HALO + tm, tc), F32),
                        pltpu.VMEM((EV_CONV_TILES, HALO, tc), F32)],
        compiler_params=_params("arbitrary", "arbitrary"),
        name="even_front",
    )(x, g.reshape(1, d), wt, conv_w, conv_b.reshape(1, -1), ln_g.reshape(1, -1), ln_b.reshape(1, -1))


def _rope128(x, cos, sin_lo, sin_hi):
    half = QK_ROPE // 2
    return (x * cos + pltpu.roll(x, LANE - half, axis=1) * sin_lo
            + pltpu.roll(x, half, axis=1) * sin_hi)


def _mla_proj_kernel(cq_ref, ckv_ref, kpe_ref, qg_ref, kvg_ref, wq_ref, wkv_ref,
                     cos_ref, slo_ref, shi_ref, q_ref, kn_ref, v_ref, kp_ref, *, scale):
    cos, slo, shi = cos_ref[...], slo_ref[...], shi_ref[...]
    hq = _rms(cq_ref[...].astype(F32), qg_ref[...]).astype(BF16)
    q = jnp.dot(hq, wq_ref[...], preferred_element_type=F32) * scale
    for h in range(MLA_HEADS):
        c0 = h * QK_PAD
        q_ref[:, c0:c0 + QK_NOPE] = q[:, c0:c0 + QK_NOPE].astype(BF16)
        pe = q[:, c0 + QK_NOPE:c0 + QK_PAD]
        q_ref[:, c0 + QK_NOPE:c0 + QK_PAD] = _rope128(pe, cos, slo, shi).astype(BF16)
    hkv = _rms(ckv_ref[...].astype(F32), kvg_ref[...]).astype(BF16)
    kv = jnp.dot(hkv, wkv_ref[...], preferred_element_type=F32)
    kn_ref[...] = kv[:, :MLA_WIDTH].astype(BF16)
    v_ref[...] = kv[:, MLA_WIDTH:].astype(BF16)
    kp_ref[...] = _rope128(kpe_ref[...].astype(F32), cos, slo, shi).astype(BF16)


def _mla_proj(p, q_norm_g, kv_norm_g, wq, wkv, cos, slo, shi, *, seq, tm=512):
    t = p.shape[0]
    nsb = seq // tm
    scale = float((QK_NOPE + QK_ROPE) ** -0.5 * math.log2(math.e))

    def rows(width, col0):
        return pl.BlockSpec((tm, width), lambda i: (i, col0 // width))

    def full(a):
        return pl.BlockSpec(a.shape, lambda i: (0, 0))

    def table():
        return pl.BlockSpec((tm, LANE), lambda i: (i % nsb, 0))

    qg = q_norm_g.reshape(1, -1)
    kvg = kv_norm_g.reshape(1, -1)
    return pl.pallas_call(
        functools.partial(_mla_proj_kernel, scale=scale),
        out_shape=(jax.ShapeDtypeStruct((t, MLA_HEADS * QK_PAD), BF16),
                   jax.ShapeDtypeStruct((t, MLA_WIDTH), BF16),
                   jax.ShapeDtypeStruct((t, MLA_WIDTH), BF16),
                   jax.ShapeDtypeStruct((t, LANE), BF16)),
        grid=(t // tm,),
        in_specs=[rows(Q_RANK, MLA_CQ), rows(KV_RANK, MLA_CKV), rows(LANE, MLA_KPE),
                  full(qg), full(kvg), full(wq), full(wkv), table(), table(), table()],
        out_specs=(pl.BlockSpec((tm, MLA_HEADS * QK_PAD), lambda i: (i, 0)),
                   pl.BlockSpec((tm, MLA_WIDTH), lambda i: (i, 0)),
                   pl.BlockSpec((tm, MLA_WIDTH), lambda i: (i, 0)),
                   pl.BlockSpec((tm, LANE), lambda i: (i, 0))),
        compiler_params=_params("parallel"),
        name="mla_proj",
    )(p, p, p, qg, kvg, wq, wkv, cos, slo, shi)


def _attn_kernel(q_ref, kn_ref, kp_ref, v_ref, bz_ref, o_ref, kf_ref, vf_ref, m_ref, acc_ref,
                 s_ref, *, tq, heads):
    qi = pl.program_id(2)

    @pl.when(qi == 0)
    def _():
        for h in range(heads):
            kf_ref[h, :, :QK_NOPE] = kn_ref[:, h * QK_NOPE:(h + 1) * QK_NOPE]
            kf_ref[h, :, QK_NOPE:] = kp_ref[...]
            vf_ref[h, :, :V_HEAD] = v_ref[:, h * V_HEAD:(h + 1) * V_HEAD]
            vf_ref[h, :, V_HEAD:] = jnp.ones((vf_ref.shape[1], LANE), BF16)

    m_ref[...] = jnp.full(m_ref.shape, -jnp.inf, F32)
    acc_ref[...] = jnp.zeros(acc_ref.shape, F32)

    def scores(j, slot):
        start = pl.multiple_of(j * tq, tq)
        for h in range(heads):
            q = q_ref[:, h * QK_PAD:(h + 1) * QK_PAD]
            k = kf_ref[h, pl.ds(start, tq), :]
            s_ref[slot, h] = lax.dot_general(q, k, (((1,), (1,)), ((), ())),
                                             preferred_element_type=F32)

    def consume(j, slot, masked):
        start = pl.multiple_of(j * tq, tq)
        for h in range(heads):
            s = s_ref[slot, h]
            if masked:
                row = lax.broadcasted_iota(jnp.int32, s.shape, 0)
                col = lax.broadcasted_iota(jnp.int32, s.shape, 1)
                s = jnp.where(col <= row, s, -jnp.inf)
            m_prev = m_ref[h]
            m_new = jnp.maximum(m_prev, jnp.max(s, axis=-1, keepdims=True))
            alpha = jnp.exp2(m_prev - m_new)
            p = jnp.exp2(s - m_new[:, :1])
            pv = jnp.dot(p.astype(BF16), vf_ref[h, pl.ds(start, tq), :],
                         preferred_element_type=F32)
            acc_ref[h] = acc_ref[h] * jnp.concatenate([alpha, alpha], axis=1) + pv
            m_ref[h] = m_new

    scores(0, 0)
    pairs = qi // 2

    def body(t, carry):
        scores(2 * t + 1, 1)
        consume(2 * t, 0, False)
        scores(2 * t + 2, 0)
        consume(2 * t + 1, 1, False)
        return carry

    lax.fori_loop(0, pairs, body, 0)

    @pl.when(qi % 2 == 0)
    def _():
        consume(qi, 0, True)

    @pl.when(qi % 2 == 1)
    def _():
        scores(qi, 1)
        consume(qi - 1, 0, False)
        consume(qi, 1, True)

    for h in range(heads):
        cs = slice(h * V_HEAD, (h + 1) * V_HEAD)
        o = acc_ref[h, :, :V_HEAD] / acc_ref[h, :, V_HEAD:]
        o_ref[:, cs] = (o * _silu(bz_ref[:, cs].astype(F32))).astype(o_ref.dtype)


def _attention(q, kn, kp, v, p, *, batch, seq, tq=1024, heads=2):
    nq = seq // tq
    vw = heads * V_HEAD
    return pl.pallas_call(
        functools.partial(_attn_kernel, tq=tq, heads=heads),
        out_shape=jax.ShapeDtypeStruct((batch * seq, MLA_WIDTH), BF16),
        grid=(batch, MLA_HEADS // heads, nq),
        in_specs=[
            pl.BlockSpec((tq, heads * QK_PAD), lambda b, h, i: (b * nq + i, h)),
            pl.BlockSpec((seq, heads * QK_NOPE), lambda b, h, i: (b, h)),
            pl.BlockSpec((seq, LANE), lambda b, h, i: (b, 0)),
            pl.BlockSpec((seq, vw), lambda b, h, i: (b, h)),
            pl.BlockSpec((tq, vw), lambda b, h, i: (b * nq + i, MLA_BZ // vw + h)),
        ],
        out_specs=pl.BlockSpec((tq, vw), lambda b, h, i: (b * nq + i, h)),
        scratch_shapes=[pltpu.VMEM((heads, seq, QK_PAD), BF16),
                        pltpu.VMEM((heads, seq, V_HEAD + LANE), BF16),
                        pltpu.VMEM((heads, tq, LANE), F32),
                        pltpu.VMEM((heads, tq, V_HEAD + LANE), F32),
                        pltpu.VMEM((2, heads, tq, tq), F32)],
        compiler_params=_params("parallel", "parallel", "arbitrary"),
        name="mla_attention",
    )(q, kn, kp, v, p)


def _out_proj_kernel(*refs, n_in, emit_y):
    a_refs = refs[:n_in]
    w_refs = refs[n_in:2 * n_in]
    x_ref, g_ref = refs[2 * n_in:2 * n_in + 2]
    outs = refs[2 * n_in + 2:]
    y = x_ref[...]
    for a_ref, w_ref in zip(a_refs, w_refs):
        y = y + jnp.dot(a_ref[...], w_ref[...], preferred_element_type=F32)
    if emit_y:
        outs[0][...] = y
    h_ref = outs[-1]
    h_ref[...] = _rms(y, g_ref[...]).astype(h_ref.dtype)


def _out_proj(acts, w, x, norm_g, *, last, tm=512, name):
    m, n = x.shape
    n_in = len(acts)
    kw = w.shape[0] // n_in
    row = pl.BlockSpec((tm, n), lambda i: (i, 0))
    in_specs = [pl.BlockSpec((tm, kw), lambda i: (i, 0)) for _ in acts]
    in_specs += [pl.BlockSpec((kw, n), (lambda i, c=c: (c, 0))) for c in range(n_in)]
    in_specs += [row, pl.BlockSpec((1, n), lambda i: (0, 0))]
    args = list(acts) + [w] * n_in + [x, norm_g.reshape(1, n)]
    if last:
        out_shape, out_specs = jax.ShapeDtypeStruct((m, n), F32), row
    else:
        out_shape = (jax.ShapeDtypeStruct((m, n), F32), jax.ShapeDtypeStruct((m, n), BF16))
        out_specs = (row, row)
    return pl.pallas_call(
        functools.partial(_out_proj_kernel, n_in=n_in, emit_y=not last),
        out_shape=out_shape,
        grid=(m // tm,),
        in_specs=in_specs,
        out_specs=out_specs,
        compiler_params=_params("parallel"),
        name=name,
    )(*args)


def _hgrn_consts():
    c = HGRN_CHUNK
    t = np.arange(c)[:, None]
    s = np.arange(c)[None, :]
    masks = []
    for lvl in range(HGRN_LEVELS):
        m = 1 << lvl
        same = (t // (2 * m)) == (s // (2 * m))
        masks.append((same & (t % (2 * m) >= m) & (s % (2 * m) < m)).astype(np.float32))
    return np.stack(masks)


def _hgrn_scan(q, k, f):
    nt = HGRN_CHUNK // SUB
    row = lax.broadcasted_iota(jnp.int32, (SUB, HGRN_K), 0)

    def tiles(x):
        return [x[i * SUB:(i + 1) * SUB, :] for i in range(nt)]

    e = tiles(f)
    qs = [a * b for a, b in zip(tiles(q), e)]
    ks = tiles(k)
    levels = []
    m = 1
    while m < SUB:
        levels.append((qs, ks))
        later = (row & m) != 0
        nq, nk, ne = [], [], []
        for i in range(nt):
            down = pltpu.roll(e[i], m, axis=0)
            up = pltpu.roll(e[i], SUB - m, axis=0)
            nq.append(qs[i] * jnp.where(later, down, 1.0))
            nk.append(ks[i] * jnp.where(later, 1.0, up))
            ne.append(e[i] * jnp.where(later, down, up))
        qs, ks, e = nq, nk, ne
        m *= 2
    mt = 1
    while mt < nt:
        levels.append((qs, ks))
        nq, nk, ne = [], [], []
        for i in range(nt):
            if (i // mt) % 2:
                nq.append(qs[i] * e[i - mt])
                nk.append(ks[i])
            else:
                nq.append(qs[i])
                nk.append(ks[i] * e[i + mt])
            ne.append(e[i] * e[i + mt] if i % (2 * mt) == 0 else ne[(i // (2 * mt)) * 2 * mt])
        qs, ks, e = nq, nk, ne
        mt *= 2
    return levels, (qs, ks), e[0]


def _hgrn_kernel(q_ref, f_ref, v_ref, g_ref, lbl_ref, ng_ref, mask_ref, o_ref, st_ref,
                 *, heads, layer):
    c = HGRN_CHUNK

    @pl.when(pl.program_id(2) == 0)
    def _():
        st_ref[...] = jnp.zeros(st_ref.shape, F32)

    nt = (((1,), (1,)), ((), ()))
    tn = (((0,), (0,)), ((), ()))

    def rows(tiles):
        return jnp.concatenate(tiles, axis=0).astype(BF16)

    for h in range(heads):
        cs = slice(h * HGRN_K, (h + 1) * HGRN_K)
        logits = lbl_ref[:, cs]
        e = jnp.exp(logits - jnp.max(logits, axis=0, keepdims=True))
        sm = e / jnp.sum(e, axis=0, keepdims=True)
        lb = jnp.sum(sm[:layer + 1], axis=0, keepdims=True) - sm[0:1]

        q = q_ref[:, cs]
        v = v_ref[:, cs]
        gate = (1.0 - lb) * _sigmoid(f_ref[:, cs])
        f = lb + gate
        k = (1.0 - lb) - gate
        levels, (q_c, k_c), e_c = _hgrn_scan(q, k, f)

        vb = v.astype(BF16)
        st = st_ref[h]
        a = jnp.zeros((c, c), F32)
        for lvl, (q_l, k_l) in enumerate(levels):
            pair = lax.dot_general(rows(q_l), rows(k_l), nt, preferred_element_type=F32)
            a = a + mask_ref[lvl] * pair
        o = (lax.dot_general(rows(q_c), st.astype(BF16), nt, preferred_element_type=F32)
             + jnp.dot(a.astype(BF16), vb, preferred_element_type=F32)
             + jnp.sum(q * k, axis=-1, keepdims=True) * v)

        st_ref[h] = st * e_c[0:1, :] + lax.dot_general(vb, rows(k_c), tn,
                                                       preferred_element_type=F32)

        ms = jnp.mean(o * o, axis=-1, keepdims=True)
        on = o * lax.rsqrt(ms + EPS) * ng_ref[...]
        o_ref[:, cs] = (on * _silu(g_ref[:, cs])).astype(o_ref.dtype)


def _hgrn(p, lb_logits, norm_g, *, batch, seq, layer, heads=16):
    c = HGRN_CHUNK
    nc = seq // c
    width = heads * HGRN_K
    fdim = HGRN_HEADS * HGRN_K
    nhb = HGRN_HEADS // heads
    masks = jnp.asarray(_hgrn_consts())

    def cols(col0):
        return pl.BlockSpec((c, width), lambda b, h, i: (b * nc + i, col0 // width + h))

    def const(a):
        nd = a.ndim
        return pl.BlockSpec(a.shape, lambda b, h, i: (0,) * nd)

    depth = lb_logits.shape[0]
    ng = norm_g.reshape(1, HGRN_V)
    return pl.pallas_call(
        functools.partial(_hgrn_kernel, heads=heads, layer=layer),
        out_shape=jax.ShapeDtypeStruct((batch * seq, D_MODEL), BF16),
        grid=(batch, nhb, nc),
        in_specs=[cols(0), cols(fdim), cols(2 * fdim), cols(2 * fdim + D_MODEL),
                  pl.BlockSpec((depth, width), lambda b, h, i: (0, h)),
                  const(ng), const(masks)],
        out_specs=pl.BlockSpec((c, width), lambda b, h, i: (b * nc + i, h)),
        scratch_shapes=[pltpu.VMEM((heads, HGRN_V, HGRN_K), F32)],
        compiler_params=_params("parallel", "parallel", "arbitrary"),
        name="hgrn2",
    )(p, p, p, p, lb_logits, ng, masks)


def _rope_tables(seq):
    half = QK_ROPE // 2
    inv_freq = 1.0 / (ROPE_BASE ** (jnp.arange(0, QK_ROPE, 2, dtype=F32) / QK_ROPE))
    ang = jnp.arange(seq, dtype=F32)[:, None] * inv_freq[None, :]
    cos, sin = jnp.cos(ang), jnp.sin(ang)
    z = jnp.zeros((seq, half), F32)
    zz = jnp.zeros((seq, LANE - QK_ROPE), F32)
    cos_t = jnp.concatenate([cos, cos, zz], axis=1)
    sin_lo = jnp.concatenate([-sin, z, zz], axis=1)
    sin_hi = jnp.concatenate([z, sin, zz], axis=1)
    return cos_t, sin_lo, sin_hi


def _even_in_weight_kernel(w_ref, o_ref):
    tc = EV_TILE // 3
    for t in range(EV_CONV_TILES):
        for part in range(3):
            src = part * CONV_CH + t * tc
            dst = t * EV_TILE + part * tc
            o_ref[dst:dst + tc, :] = w_ref[src:src + tc, :].astype(BF16)
    mla0 = EV_CONV_TILES * EV_TILE
    cq0 = 3 * CONV_CH
    kpe0 = cq0 + Q_RANK + KV_RANK
    o_ref[mla0:mla0 + MLA_BZ, :] = w_ref[cq0:kpe0, :].astype(BF16)
    o_ref[mla0 + MLA_BZ:mla0 + MLA_KPE, :] = w_ref[kpe0 + QK_ROPE:, :].astype(BF16)
    o_ref[mla0 + MLA_KPE:mla0 + MLA_KPE + QK_ROPE, :] = w_ref[kpe0:kpe0 + QK_ROPE, :].astype(BF16)
    o_ref[mla0 + MLA_KPE + QK_ROPE:, :] = jnp.zeros(
        (EV_MLA_COLS - MLA_KPE - QK_ROPE, o_ref.shape[1]), BF16)


def _even_in_weight(wt, *, tc=512):
    n, d = wt.shape
    n_out = EV_CONV_TILES * EV_TILE + EV_MLA_COLS
    return pl.pallas_call(
        _even_in_weight_kernel,
        out_shape=jax.ShapeDtypeStruct((n_out, d), BF16),
        grid=(d // tc,),
        in_specs=[pl.BlockSpec((n, tc), lambda i: (0, i))],
        out_specs=pl.BlockSpec((n_out, tc), lambda i: (0, i)),
        compiler_params=_params("parallel"),
        name="even_in_weight",
    )(wt)


def _even_layer(x, h, norm_g, w_in, conv_w, conv_b, ln_g, ln_b, q_norm_g, w_uq, kv_norm_g, w_ukv,
                w_out, next_g, tables, *, batch, seq, last):
    a_out, p = _even_front(x if h is None else h, norm_g, _even_in_weight(w_in.T), conv_w, conv_b,
                           ln_g, ln_b, seq=seq)
    wq = jnp.pad(w_uq, ((0, 0), (0, 0), (0, QK_PAD - QK_NOPE - QK_ROPE)))
    wq = wq.reshape(Q_RANK, MLA_HEADS * QK_PAD).astype(BF16)
    wkv = jnp.concatenate([w_ukv[:, :, :QK_NOPE].reshape(KV_RANK, MLA_WIDTH),
                           w_ukv[:, :, QK_NOPE:].reshape(KV_RANK, MLA_WIDTH)], axis=1).astype(BF16)
    q, kn, v, kp = _mla_proj(p, q_norm_g, kv_norm_g, wq, wkv, *tables, seq=seq)
    b_out = _attention(q, kn, kp, v, p, batch=batch, seq=seq)
    return _out_proj([a_out, b_out], w_out.astype(BF16), x, next_g, last=last, name="even_out_proj")


def _odd_layer(x, h, norm_g, w_in, lb_logits, hgrn_norm_g, w_out, next_g, *, batch, seq, layer, last):
    if h is None:
        h = _rms_rows(x, norm_g)
    p = _in_proj(h, w_in, tm=1024, tn=1024, name="odd_in_proj")
    o = _hgrn(p, lb_logits, hgrn_norm_g, batch=batch, seq=seq, layer=layer)
    return _out_proj([o], w_out.astype(BF16), x, next_g, last=last, name="odd_out_proj")


def kernel(x, ev_norm_g, ev_w_in, conv_w, conv_b, conv_ln_g, conv_ln_b, mla_q_norm_g, mla_w_uq,
           mla_kv_norm_g, mla_w_ukv, ev_w_out, od_norm_g, od_w_in, hgrn_lb_logits, hgrn_norm_g,
           od_w_out, final_norm_g):
    batch, seq, d = x.shape
    depth = hgrn_lb_logits.shape[0]
    tables = _rope_tables(seq)
    y = x.reshape(batch * seq, d)
    h = None
    for l in range(depth):
        j = l // 2
        last = l == depth - 1
        if last:
            next_g = final_norm_g
        else:
            next_g = od_norm_g[(l + 1) // 2] if l % 2 == 0 else ev_norm_g[(l + 1) // 2]
        if l % 2 == 0:
            out = _even_layer(y, h, ev_norm_g[j], ev_w_in[j], conv_w[j], conv_b[j], conv_ln_g[j],
                              conv_ln_b[j], mla_q_norm_g[j], mla_w_uq[j], mla_kv_norm_g[j],
                              mla_w_ukv[j], ev_w_out[j], next_g, tables, batch=batch, seq=seq,
                              last=last)
        else:
            out = _odd_layer(y, h, od_norm_g[j], od_w_in[j], hgrn_lb_logits, hgrn_norm_g[j],
                             od_w_out[j], next_g, batch=batch, seq=seq, layer=l, last=last)
        if last:
            y = out
        else:
            y, h = out
    return y.reshape(batch, seq, d)
```

```python
import functools
import math

import numpy as np
import jax
import jax.numpy as jnp
from jax import lax
from jax.experimental import pallas as pl
from jax.experimental.pallas import tpu as pltpu

F32 = jnp.float32
BF16 = jnp.bfloat16

D_MODEL = 2048
CONV_CH = 1024
CONV_GROUP = 128
CONV_K = 31
MLA_HEADS = 8
QK_NOPE = 128
QK_ROPE = 64
V_HEAD = 128
Q_RANK = 512
KV_RANK = 512
ROPE_BASE = 10000.0
MLA_WIDTH = MLA_HEADS * V_HEAD
QK_PAD = 256
HGRN_HEADS = 16
HGRN_K = 128
HGRN_V = 128
EPS = 1e-6

LANE = 128
SUB = 8
HALO = 32
VMEM_LIMIT = 56 * 1024 * 1024

HGRN_CHUNK = 128
HGRN_LEVELS = 7


def _params(*sem):
    return pltpu.CompilerParams(dimension_semantics=sem, vmem_limit_bytes=VMEM_LIMIT)


def _sigmoid(x):
    return 1.0 / (1.0 + jnp.exp(-x))


def _silu(x):
    return x * _sigmoid(x)


def _rms(x, g):
    ms = jnp.mean(x * x, axis=-1, keepdims=True)
    return x * lax.rsqrt(ms + EPS) * g


def _rms_rows_kernel(x_ref, g_ref, o_ref):
    o_ref[...] = _rms(x_ref[...], g_ref[...]).astype(o_ref.dtype)


def _rms_rows(x, g, *, tm=512):
    m, d = x.shape
    return pl.pallas_call(
        _rms_rows_kernel,
        out_shape=jax.ShapeDtypeStruct((m, d), BF16),
        grid=(m // tm,),
        in_specs=[pl.BlockSpec((tm, d), lambda i: (i, 0)), pl.BlockSpec((1, d), lambda i: (0, 0))],
        out_specs=pl.BlockSpec((tm, d), lambda i: (i, 0)),
        compiler_params=_params("parallel"),
        name="rms_rows",
    )(x, g.reshape(1, d))


def _in_proj_kernel(h_ref, w_ref, o_ref, wb_ref):
    @pl.when(pl.program_id(1) == 0)
    def _():
        wb_ref[...] = w_ref[...].astype(BF16)

    o_ref[...] = jnp.dot(h_ref[...], wb_ref[...], preferred_element_type=F32)


def _in_proj(h, w, *, tm, tn, name):
    m, d = h.shape
    n = w.shape[1]
    return pl.pallas_call(
        _in_proj_kernel,
        out_shape=jax.ShapeDtypeStruct((m, n), F32),
        grid=(n // tn, m // tm),
        in_specs=[pl.BlockSpec((tm, d), lambda j, i: (i, 0)),
                  pl.BlockSpec((d, tn), lambda j, i: (0, j))],
        out_specs=pl.BlockSpec((tm, tn), lambda j, i: (i, j)),
        scratch_shapes=[pltpu.VMEM((d, tn), BF16)],
        compiler_params=_params("parallel", "arbitrary"),
        name=name,
    )(h, w)


EV_TILE = 3 * 256
EV_CONV_TILES = CONV_CH // 256
EV_MLA_COLS = 3 * EV_TILE
MLA_CQ, MLA_CKV, MLA_BZ, MLA_KPE = 0, 512, 1024, 2048


def _even_front_kernel(x_ref, g_ref, w_ref, cw_ref, cb_ref, lg_ref, lb_ref, a_ref, p_ref,
                       h_ref, mm_ref, u_ref, halo_ref, *, tm, rows, blocks_per_seq):
    i = pl.program_id(0)
    j = pl.program_id(1)
    tc = EV_TILE // 3

    def project():
        return lax.dot_general(h_ref[...], w_ref[...], (((1,), (1,)), ((), ())),
                               preferred_element_type=F32)

    def conv(slot, c):
        u = mm_ref[slot, :, 0:tc] * _sigmoid(mm_ref[slot, :, tc:2 * tc])
        u_ref[HALO:, :] = u
        u_ref[:HALO, :] = jnp.where(i % blocks_per_seq == 0, 0.0, halo_ref[c])
        halo_ref[c] = u[tm - HALO:, :]
        w = cw_ref[...]
        off = HALO - (CONV_K - 1)
        for r0 in range(0, tm, rows):
            y = cb_ref[...]
            for res in range(SUB):
                taps = [k for k in range(CONV_K) if (off + k) % SUB == res]
                n = rows + (SUB if res else 0)
                z = None
                for k in taps:
                    term = w[k:k + 1, :] * u_ref[pl.ds(r0 + off + k - res, n), :]
                    z = term if z is None else z + term
                y = y + z[res:res + rows, :]
            outs = []
            for c0 in range(0, tc, CONV_GROUP):
                yg = y[:, c0:c0 + CONV_GROUP]
                mu = jnp.mean(yg, axis=-1, keepdims=True)
                d = yg - mu
                var = jnp.mean(d * d, axis=-1, keepdims=True)
                outs.append(d * lax.rsqrt(var + EPS))
            yn = jnp.concatenate(outs, axis=-1) * lg_ref[...] + lb_ref[...]
            z = mm_ref[slot, pl.ds(r0, rows), 2 * tc:3 * tc]
            a_ref[pl.ds(r0, rows), :] = (_silu(yn) * _silu(z)).astype(a_ref.dtype)

    @pl.when((i == 0) & (j == 0))
    def _():
        halo_ref[...] = jnp.zeros(halo_ref.shape, F32)

    @pl.when(j == 0)
    def _():
        if x_ref.dtype == BF16:
            h_ref[...] = x_ref[...]
        else:
            h_ref[...] = _rms(x_ref[...], g_ref[...]).astype(BF16)
        mm_ref[0] = project()

    for t in range(1, EV_CONV_TILES):
        @pl.when(j == t)
        def _(t=t):
            mm_ref[t % 2] = project()
            conv((t - 1) % 2, t - 1)

    @pl.when(j == EV_CONV_TILES)
    def _():
        p_ref[...] = project().astype(p_ref.dtype)
        conv((EV_CONV_TILES - 1) % 2, EV_CONV_TILES - 1)

    @pl.when(j > EV_CONV_TILES)
    def _():
        p_ref[...] = project().astype(p_ref.dtype)


def _even_front(x, g, wt, conv_w, conv_b, ln_g, ln_b, *, seq, tm=1024, rows=32):
    m, d = x.shape
    tc = EV_TILE // 3
    n_steps = wt.shape[0] // EV_TILE
    last_conv = EV_CONV_TILES - 1

    def conv_tile(j):
        return jnp.clip(j - 1, 0, last_conv)

    def vec():
        return pl.BlockSpec((1, tc), lambda i, j: (0, conv_tile(j)))

    return pl.pallas_call(
        functools.partial(_even_front_kernel, tm=tm, rows=rows, blocks_per_seq=seq // tm),
        out_shape=(jax.ShapeDtypeStruct((m, CONV_CH), BF16),
                   jax.ShapeDtypeStruct((m, EV_MLA_COLS), BF16)),
        grid=(m // tm, n_steps),
        in_specs=[pl.BlockSpec((tm, d), lambda i, j: (i, 0)),
                  pl.BlockSpec((1, d), lambda i, j: (0, 0)),
                  pl.BlockSpec((EV_TILE, d), lambda i, j: (j, 0)),
                  pl.BlockSpec((CONV_K, tc), lambda i, j: (0, conv_tile(j))),
                  vec(), vec(), vec()],
        out_specs=(pl.BlockSpec((tm, tc), lambda i, j: (i, conv_tile(j))),
                   pl.BlockSpec((tm, EV_TILE), lambda i, j: (i, jnp.maximum(j - EV_CONV_TILES, 0)))),
        scratch_shapes=[pltpu.VMEM((tm, d), BF16),
                        pltpu.VMEM((2, tm, EV_TILE), F32),
                        pltpu.VMEM((HALO + tm, tc), F32),
                        pltpu.VMEM((EV_CONV_TILES, HALO, tc), F32)],
        compiler_params=_params("arbitrary", "arbitrary"),
        name="even_front",
    )(x, g.reshape(1, d), wt, conv_w, conv_b.reshape(1, -1), ln_g.reshape(1, -1), ln_b.reshape(1, -1))


def _rope128(x, cos, sin_lo, sin_hi):
    half = QK_ROPE // 2
    return (x * cos + pltpu.roll(x, LANE - half, axis=1) * sin_lo
            + pltpu.roll(x, half, axis=1) * sin_hi)


def _mla_proj_kernel(cq_ref, ckv_ref, kpe_ref, qg_ref, kvg_ref, wq_ref, wkv_ref,
                     cos_ref, slo_ref, shi_ref, q_ref, kn_ref, v_ref, kp_ref, *, scale):
    cos, slo, shi = cos_ref[...], slo_ref[...], shi_ref[...]
    hq = _rms(cq_ref[...].astype(F32), qg_ref[...]).astype(BF16)
    q = jnp.dot(hq, wq_ref[...], preferred_element_type=F32) * scale
    for h in range(MLA_HEADS):
        c0 = h * QK_PAD
        q_ref[:, c0:c0 + QK_NOPE] = q[:, c0:c0 + QK_NOPE].astype(BF16)
        pe = q[:, c0 + QK_NOPE:c0 + QK_PAD]
        q_ref[:, c0 + QK_NOPE:c0 + QK_PAD] = _rope128(pe, cos, slo, shi).astype(BF16)
    hkv = _rms(ckv_ref[...].astype(F32), kvg_ref[...]).astype(BF16)
    kv = jnp.dot(hkv, wkv_ref[...], preferred_element_type=F32)
    kn_ref[...] = kv[:, :MLA_WIDTH].astype(BF16)
    v_ref[...] = kv[:, MLA_WIDTH:].astype(BF16)
    kp_ref[...] = _rope128(kpe_ref[...].astype(F32), cos, slo, shi).astype(BF16)


def _mla_proj(p, q_norm_g, kv_norm_g, wq, wkv, cos, slo, shi, *, seq, tm=512):
    t = p.shape[0]
    nsb = seq // tm
    scale = float((QK_NOPE + QK_ROPE) ** -0.5 * math.log2(math.e))

    def rows(width, col0):
        return pl.BlockSpec((tm, width), lambda i: (i, col0 // width))

    def full(a):
        return pl.BlockSpec(a.shape, lambda i: (0, 0))

    def table():
        return pl.BlockSpec((tm, LANE), lambda i: (i % nsb, 0))

    qg = q_norm_g.reshape(1, -1)
    kvg = kv_norm_g.reshape(1, -1)
    return pl.pallas_call(
        functools.partial(_mla_proj_kernel, scale=scale),
        out_shape=(jax.ShapeDtypeStruct((t, MLA_HEADS * QK_PAD), BF16),
                   jax.ShapeDtypeStruct((t, MLA_WIDTH), BF16),
                   jax.ShapeDtypeStruct((t, MLA_WIDTH), BF16),
                   jax.ShapeDtypeStruct((t, LANE), BF16)),
        grid=(t // tm,),
        in_specs=[rows(Q_RANK, MLA_CQ), rows(KV_RANK, MLA_CKV), rows(LANE, MLA_KPE),
                  full(qg), full(kvg), full(wq), full(wkv), table(), table(), table()],
        out_specs=(pl.BlockSpec((tm, MLA_HEADS * QK_PAD), lambda i: (i, 0)),
                   pl.BlockSpec((tm, MLA_WIDTH), lambda i: (i, 0)),
                   pl.BlockSpec((tm, MLA_WIDTH), lambda i: (i, 0)),
                   pl.BlockSpec((tm, LANE), lambda i: (i, 0))),
        compiler_params=_params("parallel"),
        name="mla_proj",
    )(p, p, p, qg, kvg, wq, wkv, cos, slo, shi)


def _attn_kernel(q_ref, kn_ref, kp_ref, v_ref, bz_ref, o_ref, kf_ref, vf_ref, m_ref, acc_ref,
                 s_ref, *, tq, heads):
    qi = pl.program_id(2)

    @pl.when(qi == 0)
    def _():
        for h in range(heads):
            kf_ref[h, :, :QK_NOPE] = kn_ref[:, h * QK_NOPE:(h + 1) * QK_NOPE]
            kf_ref[h, :, QK_NOPE:] = kp_ref[...]
            vf_ref[h, :, :V_HEAD] = v_ref[:, h * V_HEAD:(h + 1) * V_HEAD]
            vf_ref[h, :, V_HEAD:] = jnp.ones((vf_ref.shape[1], LANE), BF16)

    m_ref[...] = jnp.full(m_ref.shape, -jnp.inf, F32)
    acc_ref[...] = jnp.zeros(acc_ref.shape, F32)

    def scores(j, slot):
        start = pl.multiple_of(j * tq, tq)
        for h in range(heads):
            q = q_ref[:, h * QK_PAD:(h + 1) * QK_PAD]
            k = kf_ref[h, pl.ds(start, tq), :]
            s_ref[slot, h] = lax.dot_general(q, k, (((1,), (1,)), ((), ())),
                                             preferred_element_type=F32)

    def consume(j, slot, masked):
        start = pl.multiple_of(j * tq, tq)
        for h in range(heads):
            s = s_ref[slot, h]
            if masked:
                row = lax.broadcasted_iota(jnp.int32, s.shape, 0)
                col = lax.broadcasted_iota(jnp.int32, s.shape, 1)
                s = jnp.where(col <= row, s, -jnp.inf)
            m_prev = m_ref[h]
            m_new = jnp.maximum(m_prev, jnp.max(s, axis=-1, keepdims=True))
            alpha = jnp.exp2(m_prev - m_new)
            p = jnp.exp2(s - m_new[:, :1])
            pv = jnp.dot(p.astype(BF16), vf_ref[h, pl.ds(start, tq), :],
                         preferred_element_type=F32)
            acc_ref[h] = acc_ref[h] * jnp.concatenate([alpha, alpha], axis=1) + pv
            m_ref[h] = m_new

    scores(0, 0)
    pairs = qi // 2

    def body(t, carry):
        scores(2 * t + 1, 1)
        consume(2 * t, 0, False)
        scores(2 * t + 2, 0)
        consume(2 * t + 1, 1, False)
        return carry

    lax.fori_loop(0, pairs, body, 0)

    @pl.when(qi % 2 == 0)
    def _():
        consume(qi, 0, True)

    @pl.when(qi % 2 == 1)
    def _():
        scores(qi, 1)
        consume(qi - 1, 0, False)
        consume(qi, 1, True)

    for h in range(heads):
        cs = slice(h * V_HEAD, (h + 1) * V_HEAD)
        o = acc_ref[h, :, :V_HEAD] / acc_ref[h, :, V_HEAD:]
        o_ref[:, cs] = (o * _silu(bz_ref[:, cs].astype(F32))).astype(o_ref.dtype)


def _attention(q, kn, kp, v, p, *, batch, seq, tq=1024, heads=2):
    nq = seq // tq
    vw = heads * V_HEAD
    return pl.pallas_call(
        functools.partial(_attn_kernel, tq=tq, heads=heads),
        out_shape=jax.ShapeDtypeStruct((batch * seq, MLA_WIDTH), BF16),
        grid=(batch, MLA_HEADS // heads, nq),
        in_specs=[
            pl.BlockSpec((tq, heads * QK_PAD), lambda b, h, i: (b * nq + i, h)),
            pl.BlockSpec((seq, heads * QK_NOPE), lambda b, h, i: (b, h)),
            pl.BlockSpec((seq, LANE), lambda b, h, i: (b, 0)),
            pl.BlockSpec((seq, vw), lambda b, h, i: (b, h)),
            pl.BlockSpec((tq, vw), lambda b, h, i: (b * nq + i, MLA_BZ // vw + h)),
        ],
        out_specs=pl.BlockSpec((tq, vw), lambda b, h, i: (b * nq + i, h)),
        scratch_shapes=[pltpu.VMEM((heads, seq, QK_PAD), BF16),
                        pltpu.VMEM((heads, seq, V_HEAD + LANE), BF16),
                        pltpu.VMEM((heads, tq, LANE), F32),
                        pltpu.VMEM((heads, tq, V_HEAD + LANE), F32),
                        pltpu.VMEM((2, heads, tq, tq), F32)],
        compiler_params=_params("parallel", "parallel", "arbitrary"),
        name="mla_attention",
    )(q, kn, kp, v, p)


def _out_proj_kernel(*refs, n_in, emit_y):
    a_refs = refs[:n_in]
    w_refs = refs[n_in:2 * n_in]
    x_ref, g_ref = refs[2 * n_in:2 * n_in + 2]
    outs = refs[2 * n_in + 2:]
    y = x_ref[...]
    for a_ref, w_ref in zip(a_refs, w_refs):
        y = y + jnp.dot(a_ref[...], w_ref[...], preferred_element_type=F32)
    if emit_y:
        outs[0][...] = y
    h_ref = outs[-1]
    h_ref[...] = _rms(y, g_ref[...]).astype(h_ref.dtype)


def _out_proj(acts, w, x, norm_g, *, last, tm=512, name):
    m, n = x.shape
    n_in = len(acts)
    kw = w.shape[0] // n_in
    row = pl.BlockSpec((tm, n), lambda i: (i, 0))
    in_specs = [pl.BlockSpec((tm, kw), lambda i: (i, 0)) for _ in acts]
    in_specs += [pl.BlockSpec((kw, n), (lambda i, c=c: (c, 0))) for c in range(n_in)]
    in_specs += [row, pl.BlockSpec((1, n), lambda i: (0, 0))]
    args = list(acts) + [w] * n_in + [x, norm_g.reshape(1, n)]
    if last:
        out_shape, out_specs = jax.ShapeDtypeStruct((m, n), F32), row
    else:
        out_shape = (jax.ShapeDtypeStruct((m, n), F32), jax.ShapeDtypeStruct((m, n), BF16))
        out_specs = (row, row)
    return pl.pallas_call(
        functools.partial(_out_proj_kernel, n_in=n_in, emit_y=not last),
        out_shape=out_shape,
        grid=(m // tm,),
        in_specs=in_specs,
        out_specs=out_specs,
        compiler_params=_params("parallel"),
        name=name,
    )(*args)


def _hgrn_consts():
    c = HGRN_CHUNK
    t = np.arange(c)[:, None]
    s = np.arange(c)[None, :]
    masks = []
    for lvl in range(HGRN_LEVELS):
        m = 1 << lvl
        same = (t // (2 * m)) == (s // (2 * m))
        masks.append((same & (t % (2 * m) >= m) & (s % (2 * m) < m)).astype(np.float32))
    return np.stack(masks)


def _hgrn_scan(q, k, f):
    nt = HGRN_CHUNK // SUB
    row = lax.broadcasted_iota(jnp.int32, (SUB, HGRN_K), 0)

    def tiles(x):
        return [x[i * SUB:(i + 1) * SUB, :] for i in range(nt)]

    e = tiles(f)
    qs = [a * b for a, b in zip(tiles(q), e)]
    ks = tiles(k)
    levels = []
    m = 1
    while m < SUB:
        levels.append((qs, ks))
        later = (row & m) != 0
        nq, nk, ne = [], [], []
        for i in range(nt):
            down = pltpu.roll(e[i], m, axis=0)
            up = pltpu.roll(e[i], SUB - m, axis=0)
            nq.append(qs[i] * jnp.where(later, down, 1.0))
            nk.append(ks[i] * jnp.where(later, 1.0, up))
            ne.append(e[i] * jnp.where(later, down, up))
        qs, ks, e = nq, nk, ne
        m *= 2
    mt = 1
    while mt < nt:
        levels.append((qs, ks))
        nq, nk, ne = [], [], []
        for i in range(nt):
            if (i // mt) % 2:
                nq.append(qs[i] * e[i - mt])
                nk.append(ks[i])
            else:
                nq.append(qs[i])
                nk.append(ks[i] * e[i + mt])
            ne.append(e[i] * e[i + mt] if i % (2 * mt) == 0 else ne[(i // (2 * mt)) * 2 * mt])
        qs, ks, e = nq, nk, ne
        mt *= 2
    return levels, (qs, ks), e[0]


def _hgrn_kernel(q_ref, f_ref, v_ref, g_ref, lbl_ref, ng_ref, mask_ref, o_ref, st_ref,
                 *, heads, chunks, layer):
    c = HGRN_CHUNK

    @pl.when(pl.program_id(2) == 0)
    def _():
        st_ref[...] = jnp.zeros(st_ref.shape, F32)

    nt = (((1,), (1,)), ((), ()))
    tn = (((0,), (0,)), ((), ()))

    def rows(tiles):
        return jnp.concatenate(tiles, axis=0).astype(BF16)

    for h in range(heads):
        cs = slice(h * HGRN_K, (h + 1) * HGRN_K)
        logits = lbl_ref[:, cs]
        e = jnp.exp(logits - jnp.max(logits, axis=0, keepdims=True))
        sm = e / jnp.sum(e, axis=0, keepdims=True)
        lb = jnp.sum(sm[:layer + 1], axis=0, keepdims=True) - sm[0:1]

        st = st_ref[h]
        for ci in range(chunks):
            rs = slice(ci * c, (ci + 1) * c)
            q = q_ref[rs, cs]
            v = v_ref[rs, cs]
            gate = (1.0 - lb) * _sigmoid(f_ref[rs, cs])
            f = lb + gate
            k = (1.0 - lb) - gate
            levels, (q_c, k_c), e_c = _hgrn_scan(q, k, f)

            vb = v.astype(BF16)
            a = jnp.zeros((c, c), F32)
            for lvl, (q_l, k_l) in enumerate(levels):
                pair = lax.dot_general(rows(q_l), rows(k_l), nt, preferred_element_type=F32)
                a = a + mask_ref[lvl] * pair
            o = (lax.dot_general(rows(q_c), st.astype(BF16), nt, preferred_element_type=F32)
                 + jnp.dot(a.astype(BF16), vb, preferred_element_type=F32)
                 + jnp.sum(q * k, axis=-1, keepdims=True) * v)
            st = st * e_c[0:1, :] + lax.dot_general(vb, rows(k_c), tn, preferred_element_type=F32)

            ms = jnp.mean(o * o, axis=-1, keepdims=True)
            on = o * lax.rsqrt(ms + EPS) * ng_ref[...]
            o_ref[rs, cs] = (on * _silu(g_ref[rs, cs])).astype(o_ref.dtype)
        st_ref[h] = st


def _hgrn(p, lb_logits, norm_g, *, batch, seq, layer, heads=16, chunks=2):
    c = chunks * HGRN_CHUNK
    nc = seq // c
    width = heads * HGRN_K
    fdim = HGRN_HEADS * HGRN_K
    nhb = HGRN_HEADS // heads
    masks = jnp.asarray(_hgrn_consts())

    def cols(col0):
        return pl.BlockSpec((c, width), lambda b, h, i: (b * nc + i, col0 // width + h))

    def const(a):
        nd = a.ndim
        return pl.BlockSpec(a.shape, lambda b, h, i: (0,) * nd)

    depth = lb_logits.shape[0]
    ng = norm_g.reshape(1, HGRN_V)
    return pl.pallas_call(
        functools.partial(_hgrn_kernel, heads=heads, chunks=chunks, layer=layer),
        out_shape=jax.ShapeDtypeStruct((batch * seq, D_MODEL), BF16),
        grid=(batch, nhb, nc),
        in_specs=[cols(0), cols(fdim), cols(2 * fdim), cols(2 * fdim + D_MODEL),
                  pl.BlockSpec((depth, width), lambda b, h, i: (0, h)),
                  const(ng), const(masks)],
        out_specs=pl.BlockSpec((c, width), lambda b, h, i: (b * nc + i, h)),
        scratch_shapes=[pltpu.VMEM((heads, HGRN_V, HGRN_K), F32)],
        compiler_params=_params("parallel", "parallel", "arbitrary"),
        name="hgrn2",
    )(p, p, p, p, lb_logits, ng, masks)


def _rope_tables(seq):
    half = QK_ROPE // 2
    inv_freq = 1.0 / (ROPE_BASE ** (jnp.arange(0, QK_ROPE, 2, dtype=F32) / QK_ROPE))
    ang = jnp.arange(seq, dtype=F32)[:, None] * inv_freq[None, :]
    cos, sin = jnp.cos(ang), jnp.sin(ang)
    z = jnp.zeros((seq, half), F32)
    zz = jnp.zeros((seq, LANE - QK_ROPE), F32)
    cos_t = jnp.concatenate([cos, cos, zz], axis=1)
    sin_lo = jnp.concatenate([-sin, z, zz], axis=1)
    sin_hi = jnp.concatenate([z, sin, zz], axis=1)
    return cos_t, sin_lo, sin_hi


def _even_in_weight_kernel(w_ref, o_ref):
    tc = EV_TILE // 3
    for t in range(EV_CONV_TILES):
        for part in range(3):
            src = part * CONV_CH + t * tc
            dst = t * EV_TILE + part * tc
            o_ref[dst:dst + tc, :] = w_ref[src:src + tc, :].astype(BF16)
    mla0 = EV_CONV_TILES * EV_TILE
    cq0 = 3 * CONV_CH
    kpe0 = cq0 + Q_RANK + KV_RANK
    o_ref[mla0:mla0 + MLA_BZ, :] = w_ref[cq0:kpe0, :].astype(BF16)
    o_ref[mla0 + MLA_BZ:mla0 + MLA_KPE, :] = w_ref[kpe0 + QK_ROPE:, :].astype(BF16)
    o_ref[mla0 + MLA_KPE:mla0 + MLA_KPE + QK_ROPE, :] = w_ref[kpe0:kpe0 + QK_ROPE, :].astype(BF16)
    o_ref[mla0 + MLA_KPE + QK_ROPE:, :] = jnp.zeros(
        (EV_MLA_COLS - MLA_KPE - QK_ROPE, o_ref.shape[1]), BF16)


def _even_in_weight(wt, *, tc=512):
    n, d = wt.shape
    n_out = EV_CONV_TILES * EV_TILE + EV_MLA_COLS
    return pl.pallas_call(
        _even_in_weight_kernel,
        out_shape=jax.ShapeDtypeStruct((n_out, d), BF16),
        grid=(d // tc,),
        in_specs=[pl.BlockSpec((n, tc), lambda i: (0, i))],
        out_specs=pl.BlockSpec((n_out, tc), lambda i: (0, i)),
        compiler_params=_params("parallel"),
        name="even_in_weight",
    )(wt)


def _even_layer(x, h, norm_g, w_in, conv_w, conv_b, ln_g, ln_b, q_norm_g, w_uq, kv_norm_g, w_ukv,
                w_out, next_g, tables, *, batch, seq, last):
    a_out, p = _even_front(x if h is None else h, norm_g, _even_in_weight(w_in.T), conv_w, conv_b,
                           ln_g, ln_b, seq=seq)
    wq = jnp.pad(w_uq, ((0, 0), (0, 0), (0, QK_PAD - QK_NOPE - QK_ROPE)))
    wq = wq.reshape(Q_RANK, MLA_HEADS * QK_PAD).astype(BF16)
    wkv = jnp.concatenate([w_ukv[:, :, :QK_NOPE].reshape(KV_RANK, MLA_WIDTH),
                           w_ukv[:, :, QK_NOPE:].reshape(KV_RANK, MLA_WIDTH)], axis=1).astype(BF16)
    q, kn, v, kp = _mla_proj(p, q_norm_g, kv_norm_g, wq, wkv, *tables, seq=seq)
    b_out = _attention(q, kn, kp, v, p, batch=batch, seq=seq)
    return _out_proj([a_out, b_out], w_out.astype(BF16), x, next_g, last=last, name="even_out_proj")


def _odd_layer(x, h, norm_g, w_in, lb_logits, hgrn_norm_g, w_out, next_g, *, batch, seq, layer, last):
    if h is None:
        h = _rms_rows(x, norm_g)
    p = _in_proj(h, w_in, tm=1024, tn=1024, name="odd_in_proj")
    o = _hgrn(p, lb_logits, hgrn_norm_g, batch=batch, seq=seq, layer=layer)
    return _out_proj([o], w_out.astype(BF16), x, next_g, last=last, name="odd_out_proj")


def kernel(x, ev_norm_g, ev_w_in, conv_w, conv_b, conv_ln_g, conv_ln_b, mla_q_norm_g, mla_w_uq,
           mla_kv_norm_g, mla_w_ukv, ev_w_out, od_norm_g, od_w_in, hgrn_lb_logits, hgrn_norm_g,
           od_w_out, final_norm_g):
    batch, seq, d = x.shape
    depth = hgrn_lb_logits.shape[0]
    tables = _rope_tables(seq)
    y = x.reshape(batch * seq, d)
    h = None
    for l in range(depth):
        j = l // 2
        last = l == depth - 1
        if last:
            next_g = final_norm_g
        else:
            next_g = od_norm_g[(l + 1) // 2] if l % 2 == 0 else ev_norm_g[(l + 1) // 2]
        if l % 2 == 0:
            out = _even_layer(y, h, ev_norm_g[j], ev_w_in[j], conv_w[j], conv_b[j], conv_ln_g[j],
                              conv_ln_b[j], mla_q_norm_g[j], mla_w_uq[j], mla_kv_norm_g[j],
                              mla_w_ukv[j], ev_w_out[j], next_g, tables, batch=batch, seq=seq,
                              last=last)
        else:
            out = _odd_layer(y, h, od_norm_g[j], od_w_in[j], hgrn_lb_logits, hgrn_norm_g[j],
                             od_w_out[j], next_g, batch=batch, seq=seq, layer=l, last=last)
        if last:
            y = out
        else:
            y, h = out
    return y.reshape(batch, seq, d)
```

```python
import functools
import math

import numpy as np
import jax
import jax.numpy as jnp
from jax import lax
from jax.experimental import pallas as pl
from jax.experimental.pallas import tpu as pltpu

F32 = jnp.float32
BF16 = jnp.bfloat16

D_MODEL = 2048
CONV_CH = 1024
CONV_GROUP = 128
CONV_K = 31
MLA_HEADS = 8
QK_NOPE = 128
QK_ROPE = 64
V_HEAD = 128
Q_RANK = 512
KV_RANK = 512
ROPE_BASE = 10000.0
MLA_WIDTH = MLA_HEADS * V_HEAD
QK_PAD = 256
HGRN_HEADS = 16
HGRN_K = 128
HGRN_V = 128
EPS = 1e-6

LANE = 128
SUB = 8
HALO = 32
VMEM_LIMIT = 56 * 1024 * 1024

HGRN_CHUNK = 128
HGRN_LEVELS = 7


def _params(*sem):
    return pltpu.CompilerParams(dimension_semantics=sem, vmem_limit_bytes=VMEM_LIMIT)


def _sigmoid(x):
    return 1.0 / (1.0 + jnp.exp(-x))


def _silu(x):
    return x * _sigmoid(x)


def _rms(x, g):
    ms = jnp.mean(x * x, axis=-1, keepdims=True)
    return x * lax.rsqrt(ms + EPS) * g


def _rms_rows_kernel(x_ref, g_ref, o_ref):
    o_ref[...] = _rms(x_ref[...], g_ref[...]).astype(o_ref.dtype)


def _rms_rows(x, g, *, tm=512):
    m, d = x.shape
    return pl.pallas_call(
        _rms_rows_kernel,
        out_shape=jax.ShapeDtypeStruct((m, d), BF16),
        grid=(m // tm,),
        in_specs=[pl.BlockSpec((tm, d), lambda i: (i, 0)), pl.BlockSpec((1, d), lambda i: (0, 0))],
        out_specs=pl.BlockSpec((tm, d), lambda i: (i, 0)),
        compiler_params=_params("parallel"),
        name="rms_rows",
    )(x, g.reshape(1, d))


def _in_proj_kernel(h_ref, w_ref, o_ref, wb_ref):
    @pl.when(pl.program_id(1) == 0)
    def _():
        wb_ref[...] = w_ref[...].astype(BF16)

    o_ref[...] = jnp.dot(h_ref[...], wb_ref[...], preferred_element_type=F32)


def _in_proj(h, w, *, tm, tn, name):
    m, d = h.shape
    n = w.shape[1]
    return pl.pallas_call(
        _in_proj_kernel,
        out_shape=jax.ShapeDtypeStruct((m, n), F32),
        grid=(n // tn, m // tm),
        in_specs=[pl.BlockSpec((tm, d), lambda j, i: (i, 0)),
                  pl.BlockSpec((d, tn), lambda j, i: (0, j))],
        out_specs=pl.BlockSpec((tm, tn), lambda j, i: (i, j)),
        scratch_shapes=[pltpu.VMEM((d, tn), BF16)],
        compiler_params=_params("parallel", "arbitrary"),
        name=name,
    )(h, w)


EV_TILE = 3 * 256
EV_CONV_TILES = CONV_CH // 256
EV_MLA_COLS = 3 * EV_TILE
MLA_CQ, MLA_CKV, MLA_BZ, MLA_KPE = 0, 512, 1024, 2048


def _even_front_kernel(x_ref, g_ref, w_ref, cw_ref, cb_ref, lg_ref, lb_ref, a_ref, p_ref,
                       h_ref, mm_ref, u_ref, halo_ref, *, tm, rows, blocks_per_seq):
    i = pl.program_id(0)
    j = pl.program_id(1)
    tc = EV_TILE // 3

    def project():
        return lax.dot_general(h_ref[...], w_ref[...], (((1,), (1,)), ((), ())),
                               preferred_element_type=F32)

    def conv(slot, c):
        u = mm_ref[slot, :, 0:tc] * _sigmoid(mm_ref[slot, :, tc:2 * tc])
        u_ref[HALO:, :] = u
        u_ref[:HALO, :] = jnp.where(i % blocks_per_seq == 0, 0.0, halo_ref[c])
        halo_ref[c] = u[tm - HALO:, :]
        w = cw_ref[...]
        off = HALO - (CONV_K - 1)
        for r0 in range(0, tm, rows):
            y = cb_ref[...]
            for res in range(SUB):
                taps = [k for k in range(CONV_K) if (off + k) % SUB == res]
                n = rows + (SUB if res else 0)
                z = None
                for k in taps:
                    term = w[k:k + 1, :] * u_ref[pl.ds(r0 + off + k - res, n), :]
                    z = term if z is None else z + term
                y = y + z[res:res + rows, :]
            outs = []
            for c0 in range(0, tc, CONV_GROUP):
                yg = y[:, c0:c0 + CONV_GROUP]
                mu = jnp.mean(yg, axis=-1, keepdims=True)
                d = yg - mu
                var = jnp.mean(d * d, axis=-1, keepdims=True)
                outs.append(d * lax.rsqrt(var + EPS))
            yn = jnp.concatenate(outs, axis=-1) * lg_ref[...] + lb_ref[...]
            z = mm_ref[slot, pl.ds(r0, rows), 2 * tc:3 * tc]
            a_ref[pl.ds(r0, rows), :] = (_silu(yn) * _silu(z)).astype(a_ref.dtype)

    @pl.when((i == 0) & (j == 0))
    def _():
        halo_ref[...] = jnp.zeros(halo_ref.shape, F32)

    @pl.when(j == 0)
    def _():
        if x_ref.dtype == BF16:
            h_ref[...] = x_ref[...]
        else:
            h_ref[...] = _rms(x_ref[...], g_ref[...]).astype(BF16)
        mm_ref[0] = project()

    for t in range(1, EV_CONV_TILES):
        @pl.when(j == t)
        def _(t=t):
            mm_ref[t % 2] = project()
            conv((t - 1) % 2, t - 1)

    @pl.when(j == EV_CONV_TILES)
    def _():
        p_ref[...] = project().astype(p_ref.dtype)
        conv((EV_CONV_TILES - 1) % 2, EV_CONV_TILES - 1)

    @pl.when(j > EV_CONV_TILES)
    def _():
        p_ref[...] = project().astype(p_ref.dtype)


def _even_front(x, g, wt, conv_w, conv_b, ln_g, ln_b, *, seq, tm=1024, rows=32):
    m, d = x.shape
    tc = EV_TILE // 3
    n_steps = wt.shape[0] // EV_TILE
    last_conv = EV_CONV_TILES - 1

    def conv_tile(j):
        return jnp.clip(j - 1, 0, last_conv)

    def vec():
        return pl.BlockSpec((1, tc), lambda i, j: (0, conv_tile(j)))

    return pl.pallas_call(
        functools.partial(_even_front_kernel, tm=tm, rows=rows, blocks_per_seq=seq // tm),
        out_shape=(jax.ShapeDtypeStruct((m, CONV_CH), BF16),
                   jax.ShapeDtypeStruct((m, EV_MLA_COLS), BF16)),
        grid=(m // tm, n_steps),
        in_specs=[pl.BlockSpec((tm, d), lambda i, j: (i, 0)),
                  pl.BlockSpec((1, d), lambda i, j: (0, 0)),
                  pl.BlockSpec((EV_TILE, d), lambda i, j: (j, 0)),
                  pl.BlockSpec((CONV_K, tc), lambda i, j: (0, conv_tile(j))),
                  vec(), vec(), vec()],
        out_specs=(pl.BlockSpec((tm, tc), lambda i, j: (i, conv_tile(j))),
                   pl.BlockSpec((tm, EV_TILE), lambda i, j: (i, jnp.maximum(j - EV_CONV_TILES, 0)))),
        scratch_shapes=[pltpu.VMEM((tm, d), BF16),
                        pltpu.VMEM((2, tm, EV_TILE), F32),
                        pltpu.VMEM((HALO + tm, tc), F32),
                        pltpu.VMEM((EV_CONV_TILES, HALO, tc), F32)],
        compiler_params=_params("arbitrary", "arbitrary"),
        name="even_front",
    )(x, g.reshape(1, d), wt, conv_w, conv_b.reshape(1, -1), ln_g.reshape(1, -1), ln_b.reshape(1, -1))


def _rope128(x, cos, sin_lo, sin_hi):
    half = QK_ROPE // 2
    return (x * cos + pltpu.roll(x, LANE - half, axis=1) * sin_lo
            + pltpu.roll(x, half, axis=1) * sin_hi)


def _mla_proj_kernel(cq_ref, ckv_ref, kpe_ref, qg_ref, kvg_ref, wq_ref, wkv_ref,
                     cos_ref, slo_ref, shi_ref, q_ref, kn_ref, v_ref, kp_ref, *, scale):
    cos, slo, shi = cos_ref[...], slo_ref[...], shi_ref[...]
    hq = _rms(cq_ref[...].astype(F32), qg_ref[...]).astype(BF16)
    q = jnp.dot(hq, wq_ref[...], preferred_element_type=F32) * scale
    for h in range(MLA_HEADS):
        c0 = h * QK_PAD
        q_ref[:, c0:c0 + QK_NOPE] = q[:, c0:c0 + QK_NOPE].astype(BF16)
        pe = q[:, c0 + QK_NOPE:c0 + QK_PAD]
        q_ref[:, c0 + QK_NOPE:c0 + QK_PAD] = _rope128(pe, cos, slo, shi).astype(BF16)
    hkv = _rms(ckv_ref[...].astype(F32), kvg_ref[...]).astype(BF16)
    kv = jnp.dot(hkv, wkv_ref[...], preferred_element_type=F32)
    kn_ref[...] = kv[:, :MLA_WIDTH].astype(BF16)
    v_ref[...] = kv[:, MLA_WIDTH:].astype(BF16)
    kp_ref[...] = _rope128(kpe_ref[...].astype(F32), cos, slo, shi).astype(BF16)


def _mla_proj(p, q_norm_g, kv_norm_g, wq, wkv, cos, slo, shi, *, seq, tm=512):
    t = p.shape[0]
    nsb = seq // tm
    scale = float((QK_NOPE + QK_ROPE) ** -0.5 * math.log2(math.e))

    def rows(width, col0):
        return pl.BlockSpec((tm, width), lambda i: (i, col0 // width))

    def full(a):
        return pl.BlockSpec(a.shape, lambda i: (0, 0))

    def table():
        return pl.BlockSpec((tm, LANE), lambda i: (i % nsb, 0))

    qg = q_norm_g.reshape(1, -1)
    kvg = kv_norm_g.reshape(1, -1)
    return pl.pallas_call(
        functools.partial(_mla_proj_kernel, scale=scale),
        out_shape=(jax.ShapeDtypeStruct((t, MLA_HEADS * QK_PAD), BF16),
                   jax.ShapeDtypeStruct((t, MLA_WIDTH), BF16),
                   jax.ShapeDtypeStruct((t, MLA_WIDTH), BF16),
                   jax.ShapeDtypeStruct((t, LANE), BF16)),
        grid=(t // tm,),
        in_specs=[rows(Q_RANK, MLA_CQ), rows(KV_RANK, MLA_CKV), rows(LANE, MLA_KPE),
                  full(qg), full(kvg), full(wq), full(wkv), table(), table(), table()],
        out_specs=(pl.BlockSpec((tm, MLA_HEADS * QK_PAD), lambda i: (i, 0)),
                   pl.BlockSpec((tm, MLA_WIDTH), lambda i: (i, 0)),
                   pl.BlockSpec((tm, MLA_WIDTH), lambda i: (i, 0)),
                   pl.BlockSpec((tm, LANE), lambda i: (i, 0))),
        compiler_params=_params("parallel"),
        name="mla_proj",
    )(p, p, p, qg, kvg, wq, wkv, cos, slo, shi)


def _attn_kernel(q_ref, kn_ref, kp_ref, v_ref, bz_ref, o_ref, kf_ref, vf_ref, m_ref, acc_ref,
                 s_ref, *, tq, heads):
    qi = pl.program_id(2)

    @pl.when(qi == 0)
    def _():
        for h in range(heads):
            kf_ref[h, :, :QK_NOPE] = kn_ref[:, h * QK_NOPE:(h + 1) * QK_NOPE]
            kf_ref[h, :, QK_NOPE:] = kp_ref[...]
            vf_ref[h, :, :V_HEAD] = v_ref[:, h * V_HEAD:(h + 1) * V_HEAD]
            vf_ref[h, :, V_HEAD:] = jnp.ones((vf_ref.shape[1], LANE), BF16)

    m_ref[...] = jnp.full(m_ref.shape, -jnp.inf, F32)
    acc_ref[...] = jnp.zeros(acc_ref.shape, F32)

    def scores(j, slot):
        start = pl.multiple_of(j * tq, tq)
        for h in range(heads):
            q = q_ref[:, h * QK_PAD:(h + 1) * QK_PAD]
            k = kf_ref[h, pl.ds(start, tq), :]
            s_ref[slot, h] = lax.dot_general(q, k, (((1,), (1,)), ((), ())),
                                             preferred_element_type=F32)

    def consume(j, slot, masked):
        start = pl.multiple_of(j * tq, tq)
        for h in range(heads):
            s = s_ref[slot, h]
            if masked:
                row = lax.broadcasted_iota(jnp.int32, s.shape, 0)
                col = lax.broadcasted_iota(jnp.int32, s.shape, 1)
                s = jnp.where(col <= row, s, -jnp.inf)
            m_prev = m_ref[h]
            m_new = jnp.maximum(m_prev, jnp.max(s, axis=-1, keepdims=True))
            alpha = jnp.exp2(m_prev - m_new)
            p = jnp.exp2(s - m_new[:, :1])
            pv = jnp.dot(p.astype(BF16), vf_ref[h, pl.ds(start, tq), :],
                         preferred_element_type=F32)
            acc_ref[h] = acc_ref[h] * jnp.concatenate([alpha, alpha], axis=1) + pv
            m_ref[h] = m_new

    scores(0, 0)
    pairs = qi // 2

    def body(t, carry):
        scores(2 * t + 1, 1)
        consume(2 * t, 0, False)
        scores(2 * t + 2, 0)
        consume(2 * t + 1, 1, False)
        return carry

    lax.fori_loop(0, pairs, body, 0)

    @pl.when(qi % 2 == 0)
    def _():
        consume(qi, 0, True)

    @pl.when(qi % 2 == 1)
    def _():
        scores(qi, 1)
        consume(qi - 1, 0, False)
        consume(qi, 1, True)

    for h in range(heads):
        cs = slice(h * V_HEAD, (h + 1) * V_HEAD)
        o = acc_ref[h, :, :V_HEAD] / acc_ref[h, :, V_HEAD:]
        o_ref[:, cs] = (o * _silu(bz_ref[:, cs].astype(F32))).astype(o_ref.dtype)


def _attention(q, kn, kp, v, p, *, batch, seq, tq=1024, heads=2):
    nq = seq // tq
    vw = heads * V_HEAD
    return pl.pallas_call(
        functools.partial(_attn_kernel, tq=tq, heads=heads),
        out_shape=jax.ShapeDtypeStruct((batch * seq, MLA_WIDTH), BF16),
        grid=(batch, MLA_HEADS // heads, nq),
        in_specs=[
            pl.BlockSpec((tq, heads * QK_PAD), lambda b, h, i: (b * nq + i, h)),
            pl.BlockSpec((seq, heads * QK_NOPE), lambda b, h, i: (b, h)),
            pl.BlockSpec((seq, LANE), lambda b, h, i: (b, 0)),
            pl.BlockSpec((seq, vw), lambda b, h, i: (b, h)),
            pl.BlockSpec((tq, vw), lambda b, h, i: (b * nq + i, MLA_BZ // vw + h)),
        ],
        out_specs=pl.BlockSpec((tq, vw), lambda b, h, i: (b * nq + i, h)),
        scratch_shapes=[pltpu.VMEM((heads, seq, QK_PAD), BF16),
                        pltpu.VMEM((heads, seq, V_HEAD + LANE), BF16),
                        pltpu.VMEM((heads, tq, LANE), F32),
                        pltpu.VMEM((heads, tq, V_HEAD + LANE), F32),
                        pltpu.VMEM((2, heads, tq, tq), F32)],
        compiler_params=_params("parallel", "parallel", "arbitrary"),
        name="mla_attention",
    )(q, kn, kp, v, p)


def _out_proj_kernel(*refs, n_in, emit_y):
    a_refs = refs[:n_in]
    w_ref, x_ref, g_ref = refs[n_in:n_in + 3]
    outs = refs[n_in + 3:-1]
    wb_ref = refs[-1]
    kw = w_ref.shape[0] // n_in

    @pl.when(pl.program_id(0) == 0)
    def _():
        wb_ref[...] = w_ref[...].astype(BF16)

    y = x_ref[...]
    for c, a_ref in enumerate(a_refs):
        y = y + jnp.dot(a_ref[...], wb_ref[c * kw:(c + 1) * kw, :], preferred_element_type=F32)
    if emit_y:
        outs[0][...] = y
    h_ref = outs[-1]
    h_ref[...] = _rms(y, g_ref[...]).astype(h_ref.dtype)


def _out_proj(acts, w, x, norm_g, *, last, tm=512, name):
    m, n = x.shape
    n_in = len(acts)
    kw = w.shape[0] // n_in
    row = pl.BlockSpec((tm, n), lambda i: (i, 0))
    in_specs = [pl.BlockSpec((tm, kw), lambda i: (i, 0)) for _ in acts]
    in_specs += [pl.BlockSpec(w.shape, lambda i: (0, 0), pipeline_mode=pl.Buffered(1)),
                 row, pl.BlockSpec((1, n), lambda i: (0, 0))]
    args = list(acts) + [w, x, norm_g.reshape(1, n)]
    if last:
        out_shape, out_specs = jax.ShapeDtypeStruct((m, n), F32), row
    else:
        out_shape = (jax.ShapeDtypeStruct((m, n), F32), jax.ShapeDtypeStruct((m, n), BF16))
        out_specs = (row, row)
    return pl.pallas_call(
        functools.partial(_out_proj_kernel, n_in=n_in, emit_y=not last),
        out_shape=out_shape,
        grid=(m // tm,),
        in_specs=in_specs,
        out_specs=out_specs,
        scratch_shapes=[pltpu.VMEM(w.shape, BF16)],
        compiler_params=_params("arbitrary"),
        name=name,
    )(*args)


def _hgrn_consts():
    c = HGRN_CHUNK
    t = np.arange(c)[:, None]
    s = np.arange(c)[None, :]
    masks = []
    for lvl in range(HGRN_LEVELS):
        m = 1 << lvl
        same = (t // (2 * m)) == (s // (2 * m))
        masks.append((same & (t % (2 * m) >= m) & (s % (2 * m) < m)).astype(np.float32))
    return np.stack(masks)


def _hgrn_scan(q, k, f):
    nt = HGRN_CHUNK // SUB
    row = lax.broadcasted_iota(jnp.int32, (SUB, HGRN_K), 0)

    def tiles(x):
        return [x[i * SUB:(i + 1) * SUB, :] for i in range(nt)]

    e = tiles(f)
    qs = [a * b for a, b in zip(tiles(q), e)]
    ks = tiles(k)
    levels = []
    m = 1
    while m < SUB:
        levels.append((qs, ks))
        later = (row & m) != 0
        nq, nk, ne = [], [], []
        for i in range(nt):
            down = pltpu.roll(e[i], m, axis=0)
            up = pltpu.roll(e[i], SUB - m, axis=0)
            nq.append(qs[i] * jnp.where(later, down, 1.0))
            nk.append(ks[i] * jnp.where(later, 1.0, up))
            ne.append(e[i] * jnp.where(later, down, up))
        qs, ks, e = nq, nk, ne
        m *= 2
    mt = 1
    while mt < nt:
        levels.append((qs, ks))
        nq, nk, ne = [], [], []
        for i in range(nt):
            if (i // mt) % 2:
                nq.append(qs[i] * e[i - mt])
                nk.append(ks[i])
            else:
                nq.append(qs[i])
                nk.append(ks[i] * e[i + mt])
            ne.append(e[i] * e[i + mt] if i % (2 * mt) == 0 else ne[(i // (2 * mt)) * 2 * mt])
        qs, ks, e = nq, nk, ne
        mt *= 2
    return levels, (qs, ks), e[0]


def _hgrn_kernel(q_ref, f_ref, v_ref, g_ref, lbl_ref, ng_ref, mask_ref, o_ref, st_ref,
                 *, heads, chunks, layer):
    c = HGRN_CHUNK

    @pl.when(pl.program_id(2) == 0)
    def _():
        st_ref[...] = jnp.zeros(st_ref.shape, F32)

    nt = (((1,), (1,)), ((), ()))
    tn = (((0,), (0,)), ((), ()))

    def rows(tiles):
        return jnp.concatenate(tiles, axis=0).astype(BF16)

    for h in range(heads):
        cs = slice(h * HGRN_K, (h + 1) * HGRN_K)
        logits = lbl_ref[:, cs]
        e = jnp.exp(logits - jnp.max(logits, axis=0, keepdims=True))
        sm = e / jnp.sum(e, axis=0, keepdims=True)
        lb = jnp.sum(sm[:layer + 1], axis=0, keepdims=True) - sm[0:1]

        st = st_ref[h]
        for ci in range(chunks):
            rs = slice(ci * c, (ci + 1) * c)
            q = q_ref[rs, cs]
            v = v_ref[rs, cs]
            gate = (1.0 - lb) * _sigmoid(f_ref[rs, cs])
            f = lb + gate
            k = (1.0 - lb) - gate
            levels, (q_c, k_c), e_c = _hgrn_scan(q, k, f)

            vb = v.astype(BF16)
            a = jnp.zeros((c, c), F32)
            for lvl, (q_l, k_l) in enumerate(levels):
                pair = lax.dot_general(rows(q_l), rows(k_l), nt, preferred_element_type=F32)
                a = a + mask_ref[lvl] * pair
            o = (lax.dot_general(rows(q_c), st.astype(BF16), nt, preferred_element_type=F32)
                 + jnp.dot(a.astype(BF16), vb, preferred_element_type=F32)
                 + jnp.sum(q * k, axis=-1, keepdims=True) * v)
            st = st * e_c[0:1, :] + lax.dot_general(vb, rows(k_c), tn, preferred_element_type=F32)

            ms = jnp.mean(o * o, axis=-1, keepdims=True)
            on = o * lax.rsqrt(ms + EPS) * ng_ref[...]
            o_ref[rs, cs] = (on * _silu(g_ref[rs, cs])).astype(o_ref.dtype)
        st_ref[h] = st


def _hgrn(p, lb_logits, norm_g, *, batch, seq, layer, heads=16, chunks=2):
    c = chunks * HGRN_CHUNK
    nc = seq // c
    width = heads * HGRN_K
    fdim = HGRN_HEADS * HGRN_K
    nhb = HGRN_HEADS // heads
    masks = jnp.asarray(_hgrn_consts())

    def cols(col0):
        return pl.BlockSpec((c, width), lambda b, h, i: (b * nc + i, col0 // width + h))

    def const(a):
        nd = a.ndim
        return pl.BlockSpec(a.shape, lambda b, h, i: (0,) * nd)

    depth = lb_logits.shape[0]
    ng = norm_g.reshape(1, HGRN_V)
    return pl.pallas_call(
        functools.partial(_hgrn_kernel, heads=heads, chunks=chunks, layer=layer),
        out_shape=jax.ShapeDtypeStruct((batch * seq, D_MODEL), BF16),
        grid=(batch, nhb, nc),
        in_specs=[cols(0), cols(fdim), cols(2 * fdim), cols(2 * fdim + D_MODEL),
                  pl.BlockSpec((depth, width), lambda b, h, i: (0, h)),
                  const(ng), const(masks)],
        out_specs=pl.BlockSpec((c, width), lambda b, h, i: (b * nc + i, h)),
        scratch_shapes=[pltpu.VMEM((heads, HGRN_V, HGRN_K), F32)],
        compiler_params=_params("parallel", "parallel", "arbitrary"),
        name="hgrn2",
    )(p, p, p, p, lb_logits, ng, masks)


def _rope_tables(seq):
    half = QK_ROPE // 2
    inv_freq = (1.0 / (ROPE_BASE ** (np.arange(0, QK_ROPE, 2, dtype=np.float32) / QK_ROPE))
                ).astype(np.float32)
    ang = np.arange(seq, dtype=np.float32)[:, None] * inv_freq[None, :]
    cos, sin = np.cos(ang), np.sin(ang)
    z = np.zeros((seq, half), np.float32)
    zz = np.zeros((seq, LANE - QK_ROPE), np.float32)
    cos_t = np.concatenate([cos, cos, zz], axis=1)
    sin_lo = np.concatenate([-sin, z, zz], axis=1)
    sin_hi = np.concatenate([z, sin, zz], axis=1)
    return jnp.asarray(cos_t), jnp.asarray(sin_lo), jnp.asarray(sin_hi)


def _even_in_weight_kernel(w_ref, o_ref):
    tc = EV_TILE // 3
    for t in range(EV_CONV_TILES):
        for part in range(3):
            src = part * CONV_CH + t * tc
            dst = t * EV_TILE + part * tc
            o_ref[dst:dst + tc, :] = w_ref[src:src + tc, :].astype(BF16)
    mla0 = EV_CONV_TILES * EV_TILE
    cq0 = 3 * CONV_CH
    kpe0 = cq0 + Q_RANK + KV_RANK
    o_ref[mla0:mla0 + MLA_BZ, :] = w_ref[cq0:kpe0, :].astype(BF16)
    o_ref[mla0 + MLA_BZ:mla0 + MLA_KPE, :] = w_ref[kpe0 + QK_ROPE:, :].astype(BF16)
    o_ref[mla0 + MLA_KPE:mla0 + MLA_KPE + QK_ROPE, :] = w_ref[kpe0:kpe0 + QK_ROPE, :].astype(BF16)
    o_ref[mla0 + MLA_KPE + QK_ROPE:, :] = jnp.zeros(
        (EV_MLA_COLS - MLA_KPE - QK_ROPE, o_ref.shape[1]), BF16)


def _even_in_weight(wt, *, tc=512):
    n, d = wt.shape
    n_out = EV_CONV_TILES * EV_TILE + EV_MLA_COLS
    return pl.pallas_call(
        _even_in_weight_kernel,
        out_shape=jax.ShapeDtypeStruct((n_out, d), BF16),
        grid=(d // tc,),
        in_specs=[pl.BlockSpec((n, tc), lambda i: (0, i))],
        out_specs=pl.BlockSpec((n_out, tc), lambda i: (0, i)),
        compiler_params=_params("parallel"),
        name="even_in_weight",
    )(wt)


def _even_layer(x, h, norm_g, w_in, conv_w, conv_b, ln_g, ln_b, q_norm_g, w_uq, kv_norm_g, w_ukv,
                w_out, next_g, tables, *, batch, seq, last):
    a_out, p = _even_front(x if h is None else h, norm_g, _even_in_weight(w_in.T), conv_w, conv_b,
                           ln_g, ln_b, seq=seq)
    wq = jnp.pad(w_uq, ((0, 0), (0, 0), (0, QK_PAD - QK_NOPE - QK_ROPE)))
    wq = wq.reshape(Q_RANK, MLA_HEADS * QK_PAD).astype(BF16)
    wkv = jnp.concatenate([w_ukv[:, :, :QK_NOPE].reshape(KV_RANK, MLA_WIDTH),
                           w_ukv[:, :, QK_NOPE:].reshape(KV_RANK, MLA_WIDTH)], axis=1).astype(BF16)
    q, kn, v, kp = _mla_proj(p, q_norm_g, kv_norm_g, wq, wkv, *tables, seq=seq)
    b_out = _attention(q, kn, kp, v, p, batch=batch, seq=seq)
    return _out_proj([a_out, b_out], w_out, x, next_g, last=last, name="even_out_proj")


def _odd_layer(x, h, norm_g, w_in, lb_logits, hgrn_norm_g, w_out, next_g, *, batch, seq, layer, last):
    if h is None:
        h = _rms_rows(x, norm_g)
    p = _in_proj(h, w_in, tm=1024, tn=1024, name="odd_in_proj")
    o = _hgrn(p, lb_logits, hgrn_norm_g, batch=batch, seq=seq, layer=layer)
    return _out_proj([o], w_out, x, next_g, last=last, name="odd_out_proj")


def kernel(x, ev_norm_g, ev_w_in, conv_w, conv_b, conv_ln_g, conv_ln_b, mla_q_norm_g, mla_w_uq,
           mla_kv_norm_g, mla_w_ukv, ev_w_out, od_norm_g, od_w_in, hgrn_lb_logits, hgrn_norm_g,
           od_w_out, final_norm_g):
    batch, seq, d = x.shape
    depth = hgrn_lb_logits.shape[0]
    tables = _rope_tables(seq)
    y = x.reshape(batch * seq, d)
    h = None
    for l in range(depth):
        j = l // 2
        last = l == depth - 1
        if last:
            next_g = final_norm_g
        else:
            next_g = od_norm_g[(l + 1) // 2] if l % 2 == 0 else ev_norm_g[(l + 1) // 2]
        if l % 2 == 0:
            out = _even_layer(y, h, ev_norm_g[j], ev_w_in[j], conv_w[j], conv_b[j], conv_ln_g[j],
                              conv_ln_b[j], mla_q_norm_g[j], mla_w_uq[j], mla_kv_norm_g[j],
                              mla_w_ukv[j], ev_w_out[j], next_g, tables, batch=batch, seq=seq,
                              last=last)
        else:
            out = _odd_layer(y, h, od_norm_g[j], od_w_in[j], hgrn_lb_logits, hgrn_norm_g[j],
                             od_w_out[j], next_g, batch=batch, seq=seq, layer=l, last=last)
        if last:
            y = out
        else:
            y, h = out
    return y.reshape(batch, seq, d)
```

```python
import functools
import math

import numpy as np
import jax
import jax.numpy as jnp
from jax import lax
from jax.experimental import pallas as pl
from jax.experimental.pallas import tpu as pltpu

F32 = jnp.float32
BF16 = jnp.bfloat16

D_MODEL = 2048
CONV_CH = 1024
CONV_GROUP = 128
CONV_K = 31
MLA_HEADS = 8
QK_NOPE = 128
QK_ROPE = 64
V_HEAD = 128
Q_RANK = 512
KV_RANK = 512
ROPE_BASE = 10000.0
MLA_WIDTH = MLA_HEADS * V_HEAD
QK_PAD = 256
HGRN_HEADS = 16
HGRN_K = 128
HGRN_V = 128
EPS = 1e-6

LANE = 128
SUB = 8
HALO = 32
VMEM_LIMIT = 56 * 1024 * 1024

HGRN_CHUNK = 128
HGRN_LEVELS = 7

NORM_ROWS = 512
FRONT_ROWS = 1024
CONV_ROWS = 32
MLA_PROJ_ROWS = 512
ATTN_TILE = 1024
ATTN_HEADS = 2
OUT_PROJ_ROWS = 512
IN_PROJ_ROWS = 1024
IN_PROJ_COLS = 1024
HGRN_STEP_HEADS = 8
HGRN_STEP_CHUNKS = 4
WEIGHT_PREP_COLS = 512


def _params(*sem):
    return pltpu.CompilerParams(dimension_semantics=sem, vmem_limit_bytes=VMEM_LIMIT)


def _sigmoid(x):
    return 1.0 / (1.0 + jnp.exp(-x))


def _silu(x):
    return x * _sigmoid(x)


def _rms(x, g):
    ms = jnp.mean(x * x, axis=-1, keepdims=True)
    return x * lax.rsqrt(ms + EPS) * g


def _rms_rows_kernel(x_ref, g_ref, o_ref):
    o_ref[...] = _rms(x_ref[...], g_ref[...]).astype(o_ref.dtype)


def _rms_rows(x, g, *, tm=NORM_ROWS):
    m, d = x.shape
    return pl.pallas_call(
        _rms_rows_kernel,
        out_shape=jax.ShapeDtypeStruct((m, d), BF16),
        grid=(m // tm,),
        in_specs=[pl.BlockSpec((tm, d), lambda i: (i, 0)), pl.BlockSpec((1, d), lambda i: (0, 0))],
        out_specs=pl.BlockSpec((tm, d), lambda i: (i, 0)),
        compiler_params=_params("parallel"),
        name="rms_rows",
    )(x, g.reshape(1, d))


def _in_proj_kernel(h_ref, w_ref, o_ref, wb_ref):
    @pl.when(pl.program_id(1) == 0)
    def _():
        wb_ref[...] = w_ref[...].astype(BF16)

    o_ref[...] = jnp.dot(h_ref[...], wb_ref[...], preferred_element_type=F32)


def _in_proj(h, w, *, tm, tn, name):
    m, d = h.shape
    n = w.shape[1]
    return pl.pallas_call(
        _in_proj_kernel,
        out_shape=jax.ShapeDtypeStruct((m, n), F32),
        grid=(n // tn, m // tm),
        in_specs=[pl.BlockSpec((tm, d), lambda j, i: (i, 0)),
                  pl.BlockSpec((d, tn), lambda j, i: (0, j))],
        out_specs=pl.BlockSpec((tm, tn), lambda j, i: (i, j)),
        scratch_shapes=[pltpu.VMEM((d, tn), BF16)],
        compiler_params=_params("parallel", "arbitrary"),
        name=name,
    )(h, w)


EV_TILE = 3 * 256
EV_CONV_TILES = CONV_CH // 256
EV_MLA_COLS = 3 * EV_TILE
MLA_CQ, MLA_CKV, MLA_BZ, MLA_KPE = 0, 512, 1024, 2048


def _even_front_kernel(x_ref, g_ref, w_ref, cw_ref, cb_ref, lg_ref, lb_ref, a_ref, p_ref,
                       h_ref, mm_ref, u_ref, halo_ref, *, tm, rows, blocks_per_seq):
    i = pl.program_id(0)
    j = pl.program_id(1)
    tc = EV_TILE // 3

    def project():
        return lax.dot_general(h_ref[...], w_ref[...], (((1,), (1,)), ((), ())),
                               preferred_element_type=F32)

    def conv(slot, c):
        u = mm_ref[slot, :, 0:tc] * _sigmoid(mm_ref[slot, :, tc:2 * tc])
        u_ref[HALO:, :] = u
        u_ref[:HALO, :] = jnp.where(i % blocks_per_seq == 0, 0.0, halo_ref[c])
        halo_ref[c] = u[tm - HALO:, :]
        w = cw_ref[...]
        off = HALO - (CONV_K - 1)
        for r0 in range(0, tm, rows):
            y = cb_ref[...]
            for res in range(SUB):
                taps = [k for k in range(CONV_K) if (off + k) % SUB == res]
                n = rows + (SUB if res else 0)
                z = None
                for k in taps:
                    term = w[k:k + 1, :] * u_ref[pl.ds(r0 + off + k - res, n), :]
                    z = term if z is None else z + term
                y = y + z[res:res + rows, :]
            outs = []
            for c0 in range(0, tc, CONV_GROUP):
                yg = y[:, c0:c0 + CONV_GROUP]
                mu = jnp.mean(yg, axis=-1, keepdims=True)
                d = yg - mu
                var = jnp.mean(d * d, axis=-1, keepdims=True)
                outs.append(d * lax.rsqrt(var + EPS))
            yn = jnp.concatenate(outs, axis=-1) * lg_ref[...] + lb_ref[...]
            z = mm_ref[slot, pl.ds(r0, rows), 2 * tc:3 * tc]
            a_ref[pl.ds(r0, rows), :] = (_silu(yn) * _silu(z)).astype(a_ref.dtype)

    @pl.when((i == 0) & (j == 0))
    def _():
        halo_ref[...] = jnp.zeros(halo_ref.shape, F32)

    @pl.when(j == 0)
    def _():
        if x_ref.dtype == BF16:
            h_ref[...] = x_ref[...]
        else:
            h_ref[...] = _rms(x_ref[...], g_ref[...]).astype(BF16)
        mm_ref[0] = project()

    for t in range(1, EV_CONV_TILES):
        @pl.when(j == t)
        def _(t=t):
            mm_ref[t % 2] = project()
            conv((t - 1) % 2, t - 1)

    @pl.when(j == EV_CONV_TILES)
    def _():
        p_ref[...] = project().astype(p_ref.dtype)
        conv((EV_CONV_TILES - 1) % 2, EV_CONV_TILES - 1)

    @pl.when(j > EV_CONV_TILES)
    def _():
        p_ref[...] = project().astype(p_ref.dtype)


def _even_front(x, g, wt, conv_w, conv_b, ln_g, ln_b, *, seq, tm=FRONT_ROWS, rows=CONV_ROWS):
    m, d = x.shape
    tc = EV_TILE // 3
    n_steps = wt.shape[0] // EV_TILE
    last_conv = EV_CONV_TILES - 1

    def conv_tile(j):
        return jnp.clip(j - 1, 0, last_conv)

    def vec():
        return pl.BlockSpec((1, tc), lambda i, j: (0, conv_tile(j)))

    return pl.pallas_call(
        functools.partial(_even_front_kernel, tm=tm, rows=rows, blocks_per_seq=seq // tm),
        out_shape=(jax.ShapeDtypeStruct((m, CONV_CH), BF16),
                   jax.ShapeDtypeStruct((m, EV_MLA_COLS), BF16)),
        grid=(m // tm, n_steps),
        in_specs=[pl.BlockSpec((tm, d), lambda i, j: (i, 0)),
                  pl.BlockSpec((1, d), lambda i, j: (0, 0)),
                  pl.BlockSpec((EV_TILE, d), lambda i, j: (j, 0)),
                  pl.BlockSpec((CONV_K, tc), lambda i, j: (0, conv_tile(j))),
                  vec(), vec(), vec()],
        out_specs=(pl.BlockSpec((tm, tc), lambda i, j: (i, conv_tile(j))),
                   pl.BlockSpec((tm, EV_TILE), lambda i, j: (i, jnp.maximum(j - EV_CONV_TILES, 0)))),
        scratch_shapes=[pltpu.VMEM((tm, d), BF16),
                        pltpu.VMEM((2, tm, EV_TILE), F32),
                        pltpu.VMEM((HALO + tm, tc), F32),
                        pltpu.VMEM((EV_CONV_TILES, HALO, tc), F32)],
        compiler_params=_params("arbitrary", "arbitrary"),
        name="even_front",
    )(x, g.reshape(1, d), wt, conv_w, conv_b.reshape(1, -1), ln_g.reshape(1, -1), ln_b.reshape(1, -1))


def _rope128(x, cos, sin_lo, sin_hi):
    half = QK_ROPE // 2
    return (x * cos + pltpu.roll(x, LANE - half, axis=1) * sin_lo
            + pltpu.roll(x, half, axis=1) * sin_hi)


def _mla_proj_kernel(cq_ref, ckv_ref, kpe_ref, qg_ref, kvg_ref, wq_ref, wkv_ref,
                     cos_ref, slo_ref, shi_ref, q_ref, kn_ref, v_ref, kp_ref, *, scale):
    cos, slo, shi = cos_ref[...], slo_ref[...], shi_ref[...]
    hq = _rms(cq_ref[...].astype(F32), qg_ref[...]).astype(BF16)
    q = jnp.dot(hq, wq_ref[...], preferred_element_type=F32) * scale
    for h in range(MLA_HEADS):
        c0 = h * QK_PAD
        q_ref[:, c0:c0 + QK_NOPE] = q[:, c0:c0 + QK_NOPE].astype(BF16)
        pe = q[:, c0 + QK_NOPE:c0 + QK_PAD]
        q_ref[:, c0 + QK_NOPE:c0 + QK_PAD] = _rope128(pe, cos, slo, shi).astype(BF16)
    hkv = _rms(ckv_ref[...].astype(F32), kvg_ref[...]).astype(BF16)
    kv = jnp.dot(hkv, wkv_ref[...], preferred_element_type=F32)
    kn_ref[...] = kv[:, :MLA_WIDTH].astype(BF16)
    v_ref[...] = kv[:, MLA_WIDTH:].astype(BF16)
    kp_ref[...] = _rope128(kpe_ref[...].astype(F32), cos, slo, shi).astype(BF16)


def _mla_proj(p, q_norm_g, kv_norm_g, wq, wkv, cos, slo, shi, *, seq, tm=MLA_PROJ_ROWS):
    t = p.shape[0]
    nsb = seq // tm
    scale = float((QK_NOPE + QK_ROPE) ** -0.5 * math.log2(math.e))

    def rows(width, col0):
        return pl.BlockSpec((tm, width), lambda i: (i, col0 // width))

    def full(a):
        return pl.BlockSpec(a.shape, lambda i: (0, 0))

    def table():
        return pl.BlockSpec((tm, LANE), lambda i: (i % nsb, 0))

    qg = q_norm_g.reshape(1, -1)
    kvg = kv_norm_g.reshape(1, -1)
    return pl.pallas_call(
        functools.partial(_mla_proj_kernel, scale=scale),
        out_shape=(jax.ShapeDtypeStruct((t, MLA_HEADS * QK_PAD), BF16),
                   jax.ShapeDtypeStruct((t, MLA_WIDTH), BF16),
                   jax.ShapeDtypeStruct((t, MLA_WIDTH), BF16),
                   jax.ShapeDtypeStruct((t, LANE), BF16)),
        grid=(t // tm,),
        in_specs=[rows(Q_RANK, MLA_CQ), rows(KV_RANK, MLA_CKV), rows(LANE, MLA_KPE),
                  full(qg), full(kvg), full(wq), full(wkv), table(), table(), table()],
        out_specs=(pl.BlockSpec((tm, MLA_HEADS * QK_PAD), lambda i: (i, 0)),
                   pl.BlockSpec((tm, MLA_WIDTH), lambda i: (i, 0)),
                   pl.BlockSpec((tm, MLA_WIDTH), lambda i: (i, 0)),
                   pl.BlockSpec((tm, LANE), lambda i: (i, 0))),
        compiler_params=_params("parallel"),
        name="mla_proj",
    )(p, p, p, qg, kvg, wq, wkv, cos, slo, shi)


def _attn_kernel(q_ref, kn_ref, kp_ref, v_ref, bz_ref, o_ref, kf_ref, vf_ref, m_ref, acc_ref,
                 s_ref, *, tq, heads):
    qi = pl.program_id(2)

    @pl.when(qi == 0)
    def _():
        for h in range(heads):
            kf_ref[h, :, :QK_NOPE] = kn_ref[:, h * QK_NOPE:(h + 1) * QK_NOPE]
            kf_ref[h, :, QK_NOPE:] = kp_ref[...]
            vf_ref[h, :, :V_HEAD] = v_ref[:, h * V_HEAD:(h + 1) * V_HEAD]
            vf_ref[h, :, V_HEAD:] = jnp.ones((vf_ref.shape[1], LANE), BF16)

    m_ref[...] = jnp.full(m_ref.shape, -jnp.inf, F32)
    acc_ref[...] = jnp.zeros(acc_ref.shape, F32)

    def scores(j, slot):
        start = pl.multiple_of(j * tq, tq)
        for h in range(heads):
            q = q_ref[:, h * QK_PAD:(h + 1) * QK_PAD]
            k = kf_ref[h, pl.ds(start, tq), :]
            s_ref[slot, h] = lax.dot_general(q, k, (((1,), (1,)), ((), ())),
                                             preferred_element_type=F32)

    def consume(j, slot, masked):
        start = pl.multiple_of(j * tq, tq)
        for h in range(heads):
            s = s_ref[slot, h]
            if masked:
                row = lax.broadcasted_iota(jnp.int32, s.shape, 0)
                col = lax.broadcasted_iota(jnp.int32, s.shape, 1)
                s = jnp.where(col <= row, s, -jnp.inf)
            m_prev = m_ref[h]
            m_new = jnp.maximum(m_prev, jnp.max(s, axis=-1, keepdims=True))
            alpha = jnp.exp2(m_prev - m_new)
            p = jnp.exp2(s - m_new[:, :1])
            pv = jnp.dot(p.astype(BF16), vf_ref[h, pl.ds(start, tq), :],
                         preferred_element_type=F32)
            acc_ref[h] = acc_ref[h] * jnp.concatenate([alpha, alpha], axis=1) + pv
            m_ref[h] = m_new

    scores(0, 0)
    pairs = qi // 2

    def body(t, carry):
        scores(2 * t + 1, 1)
        consume(2 * t, 0, False)
        scores(2 * t + 2, 0)
        consume(2 * t + 1, 1, False)
        return carry

    lax.fori_loop(0, pairs, body, 0)

    @pl.when(qi % 2 == 0)
    def _():
        consume(qi, 0, True)

    @pl.when(qi % 2 == 1)
    def _():
        scores(qi, 1)
        consume(qi - 1, 0, False)
        consume(qi, 1, True)

    for h in range(heads):
        cs = slice(h * V_HEAD, (h + 1) * V_HEAD)
        o = acc_ref[h, :, :V_HEAD] / acc_ref[h, :, V_HEAD:]
        o_ref[:, cs] = (o * _silu(bz_ref[:, cs].astype(F32))).astype(o_ref.dtype)


def _attention(q, kn, kp, v, p, *, batch, seq, tq=ATTN_TILE, heads=ATTN_HEADS):
    nq = seq // tq
    vw = heads * V_HEAD
    return pl.pallas_call(
        functools.partial(_attn_kernel, tq=tq, heads=heads),
        out_shape=jax.ShapeDtypeStruct((batch * seq, MLA_WIDTH), BF16),
        grid=(batch, MLA_HEADS // heads, nq),
        in_specs=[
            pl.BlockSpec((tq, heads * QK_PAD), lambda b, h, i: (b * nq + i, h)),
            pl.BlockSpec((seq, heads * QK_NOPE), lambda b, h, i: (b, h)),
            pl.BlockSpec((seq, LANE), lambda b, h, i: (b, 0)),
            pl.BlockSpec((seq, vw), lambda b, h, i: (b, h)),
            pl.BlockSpec((tq, vw), lambda b, h, i: (b * nq + i, MLA_BZ // vw + h)),
        ],
        out_specs=pl.BlockSpec((tq, vw), lambda b, h, i: (b * nq + i, h)),
        scratch_shapes=[pltpu.VMEM((heads, seq, QK_PAD), BF16),
                        pltpu.VMEM((heads, seq, V_HEAD + LANE), BF16),
                        pltpu.VMEM((heads, tq, LANE), F32),
                        pltpu.VMEM((heads, tq, V_HEAD + LANE), F32),
                        pltpu.VMEM((2, heads, tq, tq), F32)],
        compiler_params=_params("parallel", "parallel", "arbitrary"),
        name="mla_attention",
    )(q, kn, kp, v, p)


def _out_proj_kernel(*refs, n_in, emit_y):
    a_refs = refs[:n_in]
    w_ref, x_ref, g_ref = refs[n_in:n_in + 3]
    outs = refs[n_in + 3:-1]
    wb_ref = refs[-1]
    kw = w_ref.shape[0] // n_in

    @pl.when(pl.program_id(0) == 0)
    def _():
        wb_ref[...] = w_ref[...].astype(BF16)

    y = x_ref[...]
    for c, a_ref in enumerate(a_refs):
        y = y + jnp.dot(a_ref[...], wb_ref[c * kw:(c + 1) * kw, :], preferred_element_type=F32)
    if emit_y:
        outs[0][...] = y
    h_ref = outs[-1]
    h_ref[...] = _rms(y, g_ref[...]).astype(h_ref.dtype)


def _out_proj(acts, w, x, norm_g, *, last, tm=OUT_PROJ_ROWS, name):
    m, n = x.shape
    n_in = len(acts)
    kw = w.shape[0] // n_in
    row = pl.BlockSpec((tm, n), lambda i: (i, 0))
    in_specs = [pl.BlockSpec((tm, kw), lambda i: (i, 0)) for _ in acts]
    in_specs += [pl.BlockSpec(w.shape, lambda i: (0, 0), pipeline_mode=pl.Buffered(1)),
                 row, pl.BlockSpec((1, n), lambda i: (0, 0))]
    args = list(acts) + [w, x, norm_g.reshape(1, n)]
    if last:
        out_shape, out_specs = jax.ShapeDtypeStruct((m, n), F32), row
    else:
        out_shape = (jax.ShapeDtypeStruct((m, n), F32), jax.ShapeDtypeStruct((m, n), BF16))
        out_specs = (row, row)
    return pl.pallas_call(
        functools.partial(_out_proj_kernel, n_in=n_in, emit_y=not last),
        out_shape=out_shape,
        grid=(m // tm,),
        in_specs=in_specs,
        out_specs=out_specs,
        scratch_shapes=[pltpu.VMEM(w.shape, BF16)],
        compiler_params=_params("arbitrary"),
        name=name,
    )(*args)


def _hgrn_consts():
    c = HGRN_CHUNK
    t = np.arange(c)[:, None]
    s = np.arange(c)[None, :]
    masks = []
    for lvl in range(HGRN_LEVELS):
        m = 1 << lvl
        same = (t // (2 * m)) == (s // (2 * m))
        masks.append((same & (t % (2 * m) >= m) & (s % (2 * m) < m)).astype(np.float32))
    return np.stack(masks)


def _hgrn_scan(q, k, f):
    nt = HGRN_CHUNK // SUB
    row = lax.broadcasted_iota(jnp.int32, (SUB, HGRN_K), 0)

    def tiles(x):
        return [x[i * SUB:(i + 1) * SUB, :] for i in range(nt)]

    e = tiles(f)
    qs = [a * b for a, b in zip(tiles(q), e)]
    ks = tiles(k)
    levels = []
    m = 1
    while m < SUB:
        levels.append((qs, ks))
        later = (row & m) != 0
        nq, nk, ne = [], [], []
        for i in range(nt):
            down = pltpu.roll(e[i], m, axis=0)
            up = pltpu.roll(e[i], SUB - m, axis=0)
            nq.append(qs[i] * jnp.where(later, down, 1.0))
            nk.append(ks[i] * jnp.where(later, 1.0, up))
            ne.append(e[i] * jnp.where(later, down, up))
        qs, ks, e = nq, nk, ne
        m *= 2
    mt = 1
    while mt < nt:
        levels.append((qs, ks))
        nq, nk, ne = [], [], []
        for i in range(nt):
            if (i // mt) % 2:
                nq.append(qs[i] * e[i - mt])
                nk.append(ks[i])
            else:
                nq.append(qs[i])
                nk.append(ks[i] * e[i + mt])
            ne.append(e[i] * e[i + mt] if i % (2 * mt) == 0 else ne[(i // (2 * mt)) * 2 * mt])
        qs, ks, e = nq, nk, ne
        mt *= 2
    return levels, (qs, ks), e[0]


def _hgrn_kernel(q_ref, f_ref, v_ref, g_ref, lbl_ref, ng_ref, mask_ref, o_ref, st_ref,
                 *, heads, chunks, layer):
    c = HGRN_CHUNK

    @pl.when(pl.program_id(2) == 0)
    def _():
        st_ref[...] = jnp.zeros(st_ref.shape, F32)

    nt = (((1,), (1,)), ((), ()))
    tn = (((0,), (0,)), ((), ()))

    def rows(tiles):
        return jnp.concatenate(tiles, axis=0).astype(BF16)

    def rows_t(tiles):
        return jnp.transpose(jnp.concatenate(tiles, axis=0)).astype(BF16)

    for h in range(heads):
        cs = slice(h * HGRN_K, (h + 1) * HGRN_K)
        logits = lbl_ref[:, cs]
        e = jnp.exp(logits - jnp.max(logits, axis=0, keepdims=True))
        sm = e / jnp.sum(e, axis=0, keepdims=True)
        lb = jnp.sum(sm[:layer + 1], axis=0, keepdims=True) - sm[0:1]

        st = st_ref[h]
        for ci in range(chunks):
            rs = slice(ci * c, (ci + 1) * c)
            q = q_ref[rs, cs]
            v = v_ref[rs, cs]
            gate = (1.0 - lb) * _sigmoid(f_ref[rs, cs])
            f = lb + gate
            k = (1.0 - lb) - gate
            levels, (q_c, k_c), e_c = _hgrn_scan(q, k, f)

            vb = v.astype(BF16)
            a = jnp.zeros((c, c), F32)
            for lvl, (q_l, k_l) in enumerate(levels):
                pair = jnp.dot(rows(q_l), rows_t(k_l), preferred_element_type=F32)
                a = a + mask_ref[lvl] * pair
            o = (lax.dot_general(rows(q_c), st.astype(BF16), nt, preferred_element_type=F32)
                 + jnp.dot(a.astype(BF16), vb, preferred_element_type=F32)
                 + jnp.sum(q * k, axis=-1, keepdims=True) * v)
            st = st * e_c[0:1, :] + lax.dot_general(vb, rows(k_c), tn, preferred_element_type=F32)

            ms = jnp.mean(o * o, axis=-1, keepdims=True)
            on = o * lax.rsqrt(ms + EPS) * ng_ref[...]
            o_ref[rs, cs] = (on * _silu(g_ref[rs, cs])).astype(o_ref.dtype)
        st_ref[h] = st


def _hgrn(p, lb_logits, norm_g, *, batch, seq, layer, heads=HGRN_STEP_HEADS,
          chunks=HGRN_STEP_CHUNKS):
    c = chunks * HGRN_CHUNK
    nc = seq // c
    width = heads * HGRN_K
    fdim = HGRN_HEADS * HGRN_K
    nhb = HGRN_HEADS // heads
    masks = jnp.asarray(_hgrn_consts())

    def cols(col0):
        return pl.BlockSpec((c, width), lambda b, h, i: (b * nc + i, col0 // width + h))

    def const(a):
        nd = a.ndim
        return pl.BlockSpec(a.shape, lambda b, h, i: (0,) * nd)

    depth = lb_logits.shape[0]
    ng = norm_g.reshape(1, HGRN_V)
    return pl.pallas_call(
        functools.partial(_hgrn_kernel, heads=heads, chunks=chunks, layer=layer),
        out_shape=jax.ShapeDtypeStruct((batch * seq, D_MODEL), BF16),
        grid=(batch, nhb, nc),
        in_specs=[cols(0), cols(fdim), cols(2 * fdim), cols(2 * fdim + D_MODEL),
                  pl.BlockSpec((depth, width), lambda b, h, i: (0, h)),
                  const(ng), const(masks)],
        out_specs=pl.BlockSpec((c, width), lambda b, h, i: (b * nc + i, h)),
        scratch_shapes=[pltpu.VMEM((heads, HGRN_V, HGRN_K), F32)],
        compiler_params=_params("parallel", "parallel", "arbitrary"),
        name="hgrn2",
    )(p, p, p, p, lb_logits, ng, masks)


def _rope_tables(seq):
    half = QK_ROPE // 2
    inv_freq = (1.0 / (ROPE_BASE ** (np.arange(0, QK_ROPE, 2, dtype=np.float32) / QK_ROPE))
                ).astype(np.float32)
    ang = np.arange(seq, dtype=np.float32)[:, None] * inv_freq[None, :]
    cos, sin = np.cos(ang), np.sin(ang)
    z = np.zeros((seq, half), np.float32)
    zz = np.zeros((seq, LANE - QK_ROPE), np.float32)
    cos_t = np.concatenate([cos, cos, zz], axis=1)
    sin_lo = np.concatenate([-sin, z, zz], axis=1)
    sin_hi = np.concatenate([z, sin, zz], axis=1)
    return jnp.asarray(cos_t), jnp.asarray(sin_lo), jnp.asarray(sin_hi)


def _even_in_weight_kernel(w_ref, o_ref):
    tc = EV_TILE // 3
    for t in range(EV_CONV_TILES):
        for part in range(3):
            src = part * CONV_CH + t * tc
            dst = t * EV_TILE + part * tc
            o_ref[dst:dst + tc, :] = w_ref[src:src + tc, :].astype(BF16)
    mla0 = EV_CONV_TILES * EV_TILE
    cq0 = 3 * CONV_CH
    kpe0 = cq0 + Q_RANK + KV_RANK
    o_ref[mla0:mla0 + MLA_BZ, :] = w_ref[cq0:kpe0, :].astype(BF16)
    o_ref[mla0 + MLA_BZ:mla0 + MLA_KPE, :] = w_ref[kpe0 + QK_ROPE:, :].astype(BF16)
    o_ref[mla0 + MLA_KPE:mla0 + MLA_KPE + QK_ROPE, :] = w_ref[kpe0:kpe0 + QK_ROPE, :].astype(BF16)
    o_ref[mla0 + MLA_KPE + QK_ROPE:, :] = jnp.zeros(
        (EV_MLA_COLS - MLA_KPE - QK_ROPE, o_ref.shape[1]), BF16)


def _even_in_weight(wt, *, tc=WEIGHT_PREP_COLS):
    n, d = wt.shape
    n_out = EV_CONV_TILES * EV_TILE + EV_MLA_COLS
    return pl.pallas_call(
        _even_in_weight_kernel,
        out_shape=jax.ShapeDtypeStruct((n_out, d), BF16),
        grid=(d // tc,),
        in_specs=[pl.BlockSpec((n, tc), lambda i: (0, i))],
        out_specs=pl.BlockSpec((n_out, tc), lambda i: (0, i)),
        compiler_params=_params("parallel"),
        name="even_in_weight",
    )(wt)


def _even_layer(x, h, norm_g, w_in, conv_w, conv_b, ln_g, ln_b, q_norm_g, w_uq, kv_norm_g, w_ukv,
                w_out, next_g, tables, *, batch, seq, last):
    a_out, p = _even_front(x if h is None else h, norm_g, _even_in_weight(w_in.T), conv_w, conv_b,
                           ln_g, ln_b, seq=seq)
    wq = jnp.pad(w_uq, ((0, 0), (0, 0), (0, QK_PAD - QK_NOPE - QK_ROPE)))
    wq = wq.reshape(Q_RANK, MLA_HEADS * QK_PAD).astype(BF16)
    wkv = jnp.concatenate([w_ukv[:, :, :QK_NOPE].reshape(KV_RANK, MLA_WIDTH),
                           w_ukv[:, :, QK_NOPE:].reshape(KV_RANK, MLA_WIDTH)], axis=1).astype(BF16)
    q, kn, v, kp = _mla_proj(p, q_norm_g, kv_norm_g, wq, wkv, *tables, seq=seq)
    b_out = _attention(q, kn, kp, v, p, batch=batch, seq=seq)
    return _out_proj([a_out, b_out], w_out, x, next_g, last=last, name="even_out_proj")


def _odd_layer(x, h, norm_g, w_in, lb_logits, hgrn_norm_g, w_out, next_g, *, batch, seq, layer, last):
    if h is None:
        h = _rms_rows(x, norm_g)
    p = _in_proj(h, w_in, tm=IN_PROJ_ROWS, tn=IN_PROJ_COLS, name="odd_in_proj")
    o = _hgrn(p, lb_logits, hgrn_norm_g, batch=batch, seq=seq, layer=layer)
    return _out_proj([o], w_out, x, next_g, last=last, name="odd_out_proj")


def kernel(x, ev_norm_g, ev_w_in, conv_w, conv_b, conv_ln_g, conv_ln_b, mla_q_norm_g, mla_w_uq,
           mla_kv_norm_g, mla_w_ukv, ev_w_out, od_norm_g, od_w_in, hgrn_lb_logits, hgrn_norm_g,
           od_w_out, final_norm_g):
    batch, seq, d = x.shape
    depth = hgrn_lb_logits.shape[0]
    tables = _rope_tables(seq)
    y = x.reshape(batch * seq, d)
    h = None
    for l in range(depth):
        j = l // 2
        last = l == depth - 1
        if last:
            next_g = final_norm_g
        else:
            next_g = od_norm_g[(l + 1) // 2] if l % 2 == 0 else ev_norm_g[(l + 1) // 2]
        if l % 2 == 0:
            out = _even_layer(y, h, ev_norm_g[j], ev_w_in[j], conv_w[j], conv_b[j], conv_ln_g[j],
                              conv_ln_b[j], mla_q_norm_g[j], mla_w_uq[j], mla_kv_norm_g[j],
                              mla_w_ukv[j], ev_w_out[j], next_g, tables, batch=batch, seq=seq,
                              last=last)
        else:
            out = _odd_layer(y, h, od_norm_g[j], od_w_in[j], hgrn_lb_logits, hgrn_norm_g[j],
                             od_w_out[j], next_g, batch=batch, seq=seq, layer=l, last=last)
        if last:
            y = out
        else:
            y, h = out
    return y.reshape(batch, seq, d)
```

```python
import functools
import math

import numpy as np
import jax
import jax.numpy as jnp
from jax import lax
from jax.experimental import pallas as pl
from jax.experimental.pallas import tpu as pltpu

F32 = jnp.float32
BF16 = jnp.bfloat16

D_MODEL = 2048
CONV_CH = 1024
CONV_GROUP = 128
CONV_K = 31
MLA_HEADS = 8
QK_NOPE = 128
QK_ROPE = 64
V_HEAD = 128
Q_RANK = 512
KV_RANK = 512
ROPE_BASE = 10000.0
MLA_WIDTH = MLA_HEADS * V_HEAD
QK_PAD = 256
HGRN_HEADS = 16
HGRN_K = 128
HGRN_V = 128
EPS = 1e-6

LANE = 128
SUB = 8
HALO = 32
VMEM_LIMIT = 56 * 1024 * 1024

HGRN_CHUNK = 128
HGRN_LEVELS = 7

NORM_ROWS = 512
FRONT_ROWS = 1024
CONV_ROWS = 32
MLA_PROJ_ROWS = 512
ATTN_TILE = 1024
ATTN_HEADS = 2
OUT_PROJ_ROWS = 512
IN_PROJ_ROWS = 1024
IN_PROJ_COLS = 1024
HGRN_STEP_HEADS = 8
HGRN_STEP_CHUNKS = 4
WEIGHT_PREP_COLS = 512


def _params(*sem):
    return pltpu.CompilerParams(dimension_semantics=sem, vmem_limit_bytes=VMEM_LIMIT)


def _sigmoid(x):
    return 1.0 / (1.0 + jnp.exp(-x))


def _silu(x):
    return x * _sigmoid(x)


def _rms(x, g):
    ms = jnp.mean(x * x, axis=-1, keepdims=True)
    return x * lax.rsqrt(ms + EPS) * g


def _rms_rows_kernel(x_ref, g_ref, o_ref):
    o_ref[...] = _rms(x_ref[...], g_ref[...]).astype(o_ref.dtype)


def _rms_rows(x, g, *, tm=NORM_ROWS):
    m, d = x.shape
    return pl.pallas_call(
        _rms_rows_kernel,
        out_shape=jax.ShapeDtypeStruct((m, d), BF16),
        grid=(m // tm,),
        in_specs=[pl.BlockSpec((tm, d), lambda i: (i, 0)), pl.BlockSpec((1, d), lambda i: (0, 0))],
        out_specs=pl.BlockSpec((tm, d), lambda i: (i, 0)),
        compiler_params=_params("parallel"),
        name="rms_rows",
    )(x, g.reshape(1, d))


def _in_proj_kernel(h_ref, w_ref, o_ref, wb_ref):
    @pl.when(pl.program_id(1) == 0)
    def _():
        wb_ref[...] = w_ref[...].astype(BF16)

    o_ref[...] = jnp.dot(h_ref[...], wb_ref[...], preferred_element_type=F32)


def _in_proj(h, w, *, tm, tn, name):
    m, d = h.shape
    n = w.shape[1]
    return pl.pallas_call(
        _in_proj_kernel,
        out_shape=jax.ShapeDtypeStruct((m, n), F32),
        grid=(n // tn, m // tm),
        in_specs=[pl.BlockSpec((tm, d), lambda j, i: (i, 0)),
                  pl.BlockSpec((d, tn), lambda j, i: (0, j))],
        out_specs=pl.BlockSpec((tm, tn), lambda j, i: (i, j)),
        scratch_shapes=[pltpu.VMEM((d, tn), BF16)],
        compiler_params=_params("parallel", "arbitrary"),
        name=name,
    )(h, w)


EV_TILE = 3 * 256
EV_CONV_TILES = CONV_CH // 256
EV_MLA_COLS = 3 * EV_TILE
MLA_CQ, MLA_CKV, MLA_BZ, MLA_KPE = 0, 512, 1024, 2048


def _even_front_kernel(x_ref, g_ref, w_ref, cw_ref, cb_ref, lg_ref, lb_ref, a_ref, p_ref,
                       h_ref, mm_ref, u_ref, halo_ref, *, tm, rows, blocks_per_seq):
    i = pl.program_id(0)
    j = pl.program_id(1)
    tc = EV_TILE // 3

    def project():
        return lax.dot_general(h_ref[...], w_ref[...], (((1,), (1,)), ((), ())),
                               preferred_element_type=F32)

    def conv(slot, c):
        u = mm_ref[slot, :, 0:tc] * _sigmoid(mm_ref[slot, :, tc:2 * tc])
        u_ref[HALO:, :] = u
        u_ref[:HALO, :] = jnp.where(i % blocks_per_seq == 0, 0.0, halo_ref[c])
        halo_ref[c] = u[tm - HALO:, :]
        w = cw_ref[...]
        off = HALO - (CONV_K - 1)
        for r0 in range(0, tm, rows):
            y = cb_ref[...]
            for res in range(SUB):
                taps = [k for k in range(CONV_K) if (off + k) % SUB == res]
                n = rows + (SUB if res else 0)
                z = None
                for k in taps:
                    term = w[k:k + 1, :] * u_ref[pl.ds(r0 + off + k - res, n), :]
                    z = term if z is None else z + term
                y = y + z[res:res + rows, :]
            outs = []
            for c0 in range(0, tc, CONV_GROUP):
                yg = y[:, c0:c0 + CONV_GROUP]
                mu = jnp.mean(yg, axis=-1, keepdims=True)
                d = yg - mu
                var = jnp.mean(d * d, axis=-1, keepdims=True)
                outs.append(d * lax.rsqrt(var + EPS))
            yn = jnp.concatenate(outs, axis=-1) * lg_ref[...] + lb_ref[...]
            z = mm_ref[slot, pl.ds(r0, rows), 2 * tc:3 * tc]
            a_ref[pl.ds(r0, rows), :] = (_silu(yn) * _silu(z)).astype(a_ref.dtype)

    @pl.when((i == 0) & (j == 0))
    def _():
        halo_ref[...] = jnp.zeros(halo_ref.shape, F32)

    @pl.when(j == 0)
    def _():
        if x_ref.dtype == BF16:
            h_ref[...] = x_ref[...]
        else:
            h_ref[...] = _rms(x_ref[...], g_ref[...]).astype(BF16)
        mm_ref[0] = project()

    for t in range(1, EV_CONV_TILES):
        @pl.when(j == t)
        def _(t=t):
            mm_ref[t % 2] = project()
            conv((t - 1) % 2, t - 1)

    @pl.when(j == EV_CONV_TILES)
    def _():
        p_ref[...] = project().astype(p_ref.dtype)
        conv((EV_CONV_TILES - 1) % 2, EV_CONV_TILES - 1)

    @pl.when(j > EV_CONV_TILES)
    def _():
        p_ref[...] = project().astype(p_ref.dtype)


def _even_front(x, g, wt, conv_w, conv_b, ln_g, ln_b, *, seq, tm=FRONT_ROWS, rows=CONV_ROWS):
    m, d = x.shape
    tc = EV_TILE // 3
    n_steps = wt.shape[0] // EV_TILE
    last_conv = EV_CONV_TILES - 1

    def conv_tile(j):
        return jnp.clip(j - 1, 0, last_conv)

    def vec():
        return pl.BlockSpec((1, tc), lambda i, j: (0, conv_tile(j)))

    return pl.pallas_call(
        functools.partial(_even_front_kernel, tm=tm, rows=rows, blocks_per_seq=seq // tm),
        out_shape=(jax.ShapeDtypeStruct((m, CONV_CH), BF16),
                   jax.ShapeDtypeStruct((m, EV_MLA_COLS), BF16)),
        grid=(m // tm, n_steps),
        in_specs=[pl.BlockSpec((tm, d), lambda i, j: (i, 0)),
                  pl.BlockSpec((1, d), lambda i, j: (0, 0)),
                  pl.BlockSpec((EV_TILE, d), lambda i, j: (j, 0)),
                  pl.BlockSpec((CONV_K, tc), lambda i, j: (0, conv_tile(j))),
                  vec(), vec(), vec()],
        out_specs=(pl.BlockSpec((tm, tc), lambda i, j: (i, conv_tile(j))),
                   pl.BlockSpec((tm, EV_TILE), lambda i, j: (i, jnp.maximum(j - EV_CONV_TILES, 0)))),
        scratch_shapes=[pltpu.VMEM((tm, d), BF16),
                        pltpu.VMEM((2, tm, EV_TILE), F32),
                        pltpu.VMEM((HALO + tm, tc), F32),
                        pltpu.VMEM((EV_CONV_TILES, HALO, tc), F32)],
        compiler_params=_params("arbitrary", "arbitrary"),
        name="even_front",
    )(x, g.reshape(1, d), wt, conv_w, conv_b.reshape(1, -1), ln_g.reshape(1, -1), ln_b.reshape(1, -1))


def _rope128(x, cos, sin_lo, sin_hi):
    half = QK_ROPE // 2
    return (x * cos + pltpu.roll(x, LANE - half, axis=1) * sin_lo
            + pltpu.roll(x, half, axis=1) * sin_hi)


def _mla_proj_kernel(cq_ref, ckv_ref, kpe_ref, qg_ref, kvg_ref, wq_ref, wkv_ref,
                     cos_ref, slo_ref, shi_ref, q_ref, kn_ref, v_ref, kp_ref, *, scale):
    cos, slo, shi = cos_ref[...], slo_ref[...], shi_ref[...]
    hq = _rms(cq_ref[...].astype(F32), qg_ref[...]).astype(BF16)
    q = jnp.dot(hq, wq_ref[...], preferred_element_type=F32) * scale
    for h in range(MLA_HEADS):
        c0 = h * QK_PAD
        q_ref[:, c0:c0 + QK_NOPE] = q[:, c0:c0 + QK_NOPE].astype(BF16)
        pe = q[:, c0 + QK_NOPE:c0 + QK_PAD]
        q_ref[:, c0 + QK_NOPE:c0 + QK_PAD] = _rope128(pe, cos, slo, shi).astype(BF16)
    hkv = _rms(ckv_ref[...].astype(F32), kvg_ref[...]).astype(BF16)
    kv = jnp.dot(hkv, wkv_ref[...], preferred_element_type=F32)
    kn_ref[...] = kv[:, :MLA_WIDTH].astype(BF16)
    v_ref[...] = kv[:, MLA_WIDTH:].astype(BF16)
    kp_ref[...] = _rope128(kpe_ref[...].astype(F32), cos, slo, shi).astype(BF16)


def _mla_proj(p, q_norm_g, kv_norm_g, wq, wkv, cos, slo, shi, *, seq, tm=MLA_PROJ_ROWS):
    t = p.shape[0]
    nsb = seq // tm
    scale = float((QK_NOPE + QK_ROPE) ** -0.5 * math.log2(math.e))

    def rows(width, col0):
        return pl.BlockSpec((tm, width), lambda i: (i, col0 // width))

    def full(a):
        return pl.BlockSpec(a.shape, lambda i: (0, 0))

    def table():
        return pl.BlockSpec((tm, LANE), lambda i: (i % nsb, 0))

    qg = q_norm_g.reshape(1, -1)
    kvg = kv_norm_g.reshape(1, -1)
    return pl.pallas_call(
        functools.partial(_mla_proj_kernel, scale=scale),
        out_shape=(jax.ShapeDtypeStruct((t, MLA_HEADS * QK_PAD), BF16),
                   jax.ShapeDtypeStruct((t, MLA_WIDTH), BF16),
                   jax.ShapeDtypeStruct((t, MLA_WIDTH), BF16),
                   jax.ShapeDtypeStruct((t, LANE), BF16)),
        grid=(t // tm,),
        in_specs=[rows(Q_RANK, MLA_CQ), rows(KV_RANK, MLA_CKV), rows(LANE, MLA_KPE),
                  full(qg), full(kvg), full(wq), full(wkv), table(), table(), table()],
        out_specs=(pl.BlockSpec((tm, MLA_HEADS * QK_PAD), lambda i: (i, 0)),
                   pl.BlockSpec((tm, MLA_WIDTH), lambda i: (i, 0)),
                   pl.BlockSpec((tm, MLA_WIDTH), lambda i: (i, 0)),
                   pl.BlockSpec((tm, LANE), lambda i: (i, 0))),
        compiler_params=_params("parallel"),
        name="mla_proj",
    )(p, p, p, qg, kvg, wq, wkv, cos, slo, shi)


def _attn_kernel(q_ref, kn_ref, kp_ref, v_ref, bz_ref, o_ref, kf_ref, vf_ref, m_ref, acc_ref,
                 s_ref, *, tq, heads):
    qi = pl.program_id(2)

    @pl.when(qi == 0)
    def _():
        for h in range(heads):
            kf_ref[h, :, :QK_NOPE] = kn_ref[:, h * QK_NOPE:(h + 1) * QK_NOPE]
            kf_ref[h, :, QK_NOPE:] = kp_ref[...]
            vf_ref[h, :, :V_HEAD] = v_ref[:, h * V_HEAD:(h + 1) * V_HEAD]
            vf_ref[h, :, V_HEAD:] = jnp.ones((vf_ref.shape[1], LANE), BF16)

    m_ref[...] = jnp.full(m_ref.shape, -jnp.inf, F32)
    acc_ref[...] = jnp.zeros(acc_ref.shape, F32)

    def scores(j, slot):
        start = pl.multiple_of(j * tq, tq)
        for h in range(heads):
            q = q_ref[:, h * QK_PAD:(h + 1) * QK_PAD]
            k = kf_ref[h, pl.ds(start, tq), :]
            s_ref[slot, h] = lax.dot_general(q, k, (((1,), (1,)), ((), ())),
                                             preferred_element_type=F32)

    def consume(j, slot, masked):
        start = pl.multiple_of(j * tq, tq)
        for h in range(heads):
            s = s_ref[slot, h]
            if masked:
                row = lax.broadcasted_iota(jnp.int32, s.shape, 0)
                col = lax.broadcasted_iota(jnp.int32, s.shape, 1)
                s = jnp.where(col <= row, s, -jnp.inf)
            m_prev = m_ref[h]
            m_new = jnp.maximum(m_prev, jnp.max(s, axis=-1, keepdims=True))
            alpha = jnp.exp2(m_prev - m_new)
            p = jnp.exp2(s - jnp.concatenate([m_new] * (tq // LANE), axis=1))
            pv = jnp.dot(p.astype(BF16), vf_ref[h, pl.ds(start, tq), :],
                         preferred_element_type=F32)
            acc_ref[h] = acc_ref[h] * jnp.concatenate([alpha, alpha], axis=1) + pv
            m_ref[h] = m_new

    scores(0, 0)
    pairs = qi // 2

    def body(t, carry):
        scores(2 * t + 1, 1)
        consume(2 * t, 0, False)
        scores(2 * t + 2, 0)
        consume(2 * t + 1, 1, False)
        return carry

    lax.fori_loop(0, pairs, body, 0)

    @pl.when(qi % 2 == 0)
    def _():
        consume(qi, 0, True)

    @pl.when(qi % 2 == 1)
    def _():
        scores(qi, 1)
        consume(qi - 1, 0, False)
        consume(qi, 1, True)

    for h in range(heads):
        cs = slice(h * V_HEAD, (h + 1) * V_HEAD)
        o = acc_ref[h, :, :V_HEAD] / acc_ref[h, :, V_HEAD:]
        o_ref[:, cs] = (o * _silu(bz_ref[:, cs].astype(F32))).astype(o_ref.dtype)


def _attention(q, kn, kp, v, p, *, batch, seq, tq=ATTN_TILE, heads=ATTN_HEADS):
    nq = seq // tq
    vw = heads * V_HEAD
    return pl.pallas_call(
        functools.partial(_attn_kernel, tq=tq, heads=heads),
        out_shape=jax.ShapeDtypeStruct((batch * seq, MLA_WIDTH), BF16),
        grid=(batch, MLA_HEADS // heads, nq),
        in_specs=[
            pl.BlockSpec((tq, heads * QK_PAD), lambda b, h, i: (b * nq + i, h)),
            pl.BlockSpec((seq, heads * QK_NOPE), lambda b, h, i: (b, h)),
            pl.BlockSpec((seq, LANE), lambda b, h, i: (b, 0)),
            pl.BlockSpec((seq, vw), lambda b, h, i: (b, h)),
            pl.BlockSpec((tq, vw), lambda b, h, i: (b * nq + i, MLA_BZ // vw + h)),
        ],
        out_specs=pl.BlockSpec((tq, vw), lambda b, h, i: (b * nq + i, h)),
        scratch_shapes=[pltpu.VMEM((heads, seq, QK_PAD), BF16),
                        pltpu.VMEM((heads, seq, V_HEAD + LANE), BF16),
                        pltpu.VMEM((heads, tq, LANE), F32),
                        pltpu.VMEM((heads, tq, V_HEAD + LANE), F32),
                        pltpu.VMEM((2, heads, tq, tq), F32)],
        compiler_params=_params("parallel", "parallel", "arbitrary"),
        name="mla_attention",
    )(q, kn, kp, v, p)


def _out_proj_kernel(*refs, n_in, emit_y):
    a_refs = refs[:n_in]
    w_ref, x_ref, g_ref = refs[n_in:n_in + 3]
    outs = refs[n_in + 3:-1]
    wb_ref = refs[-1]
    kw = w_ref.shape[0] // n_in

    @pl.when(pl.program_id(0) == 0)
    def _():
        wb_ref[...] = w_ref[...].astype(BF16)

    y = x_ref[...]
    for c, a_ref in enumerate(a_refs):
        y = y + jnp.dot(a_ref[...], wb_ref[c * kw:(c + 1) * kw, :], preferred_element_type=F32)
    if emit_y:
        outs[0][...] = y
    h_ref = outs[-1]
    h_ref[...] = _rms(y, g_ref[...]).astype(h_ref.dtype)


def _out_proj(acts, w, x, norm_g, *, last, tm=OUT_PROJ_ROWS, name):
    m, n = x.shape
    n_in = len(acts)
    kw = w.shape[0] // n_in
    row = pl.BlockSpec((tm, n), lambda i: (i, 0))
    in_specs = [pl.BlockSpec((tm, kw), lambda i: (i, 0)) for _ in acts]
    in_specs += [pl.BlockSpec(w.shape, lambda i: (0, 0), pipeline_mode=pl.Buffered(1)),
                 row, pl.BlockSpec((1, n), lambda i: (0, 0))]
    args = list(acts) + [w, x, norm_g.reshape(1, n)]
    if last:
        out_shape, out_specs = jax.ShapeDtypeStruct((m, n), F32), row
    else:
        out_shape = (jax.ShapeDtypeStruct((m, n), F32), jax.ShapeDtypeStruct((m, n), BF16))
        out_specs = (row, row)
    return pl.pallas_call(
        functools.partial(_out_proj_kernel, n_in=n_in, emit_y=not last),
        out_shape=out_shape,
        grid=(m // tm,),
        in_specs=in_specs,
        out_specs=out_specs,
        scratch_shapes=[pltpu.VMEM(w.shape, BF16)],
        compiler_params=_params("arbitrary"),
        name=name,
    )(*args)


def _hgrn_consts():
    c = HGRN_CHUNK
    t = np.arange(c)[:, None]
    s = np.arange(c)[None, :]
    masks = []
    for lvl in range(HGRN_LEVELS):
        m = 1 << lvl
        same = (t // (2 * m)) == (s // (2 * m))
        masks.append((same & (t % (2 * m) >= m) & (s % (2 * m) < m)).astype(np.float32))
    return np.stack(masks)


def _hgrn_scan(q, k, f):
    nt = HGRN_CHUNK // SUB
    row = lax.broadcasted_iota(jnp.int32, (SUB, HGRN_K), 0)

    def tiles(x):
        return [x[i * SUB:(i + 1) * SUB, :] for i in range(nt)]

    e = tiles(f)
    qs = [a * b for a, b in zip(tiles(q), e)]
    ks = tiles(k)
    levels = []
    m = 1
    while m < SUB:
        levels.append((qs, ks))
        later = (row & m) != 0
        nq, nk, ne = [], [], []
        for i in range(nt):
            down = pltpu.roll(e[i], m, axis=0)
            up = pltpu.roll(e[i], SUB - m, axis=0)
            nq.append(qs[i] * jnp.where(later, down, 1.0))
            nk.append(ks[i] * jnp.where(later, 1.0, up))
            ne.append(e[i] * jnp.where(later, down, up))
        qs, ks, e = nq, nk, ne
        m *= 2
    mt = 1
    while mt < nt:
        levels.append((qs, ks))
        nq, nk, ne = [], [], []
        for i in range(nt):
            if (i // mt) % 2:
                nq.append(qs[i] * e[i - mt])
                nk.append(ks[i])
            else:
                nq.append(qs[i])
                nk.append(ks[i] * e[i + mt])
            ne.append(e[i] * e[i + mt] if i % (2 * mt) == 0 else ne[(i // (2 * mt)) * 2 * mt])
        qs, ks, e = nq, nk, ne
        mt *= 2
    return levels, (qs, ks), e[0]


def _hgrn_kernel(q_ref, f_ref, v_ref, g_ref, lbl_ref, ng_ref, mask_ref, o_ref, st_ref,
                 *, heads, chunks, layer):
    c = HGRN_CHUNK

    @pl.when(pl.program_id(2) == 0)
    def _():
        st_ref[...] = jnp.zeros(st_ref.shape, F32)

    nt = (((1,), (1,)), ((), ()))
    tn = (((0,), (0,)), ((), ()))

    def rows(tiles):
        return jnp.concatenate(tiles, axis=0).astype(BF16)

    def rows_t(tiles):
        return jnp.transpose(jnp.concatenate(tiles, axis=0)).astype(BF16)

    for h in range(heads):
        cs = slice(h * HGRN_K, (h + 1) * HGRN_K)
        logits = lbl_ref[:, cs]
        e = jnp.exp(logits - jnp.max(logits, axis=0, keepdims=True))
        sm = e / jnp.sum(e, axis=0, keepdims=True)
        lb = jnp.sum(sm[:layer + 1], axis=0, keepdims=True) - sm[0:1]

        st = st_ref[h]
        for ci in range(chunks):
            rs = slice(ci * c, (ci + 1) * c)
            q = q_ref[rs, cs]
            v = v_ref[rs, cs]
            gate = (1.0 - lb) * _sigmoid(f_ref[rs, cs])
            f = lb + gate
            k = (1.0 - lb) - gate
            levels, (q_c, k_c), e_c = _hgrn_scan(q, k, f)

            vb = v.astype(BF16)
            a = jnp.zeros((c, c), F32)
            for lvl, (q_l, k_l) in enumerate(levels):
                pair = jnp.dot(rows(q_l), rows_t(k_l), preferred_element_type=F32)
                a = a + mask_ref[lvl] * pair
            o = (lax.dot_general(rows(q_c), st.astype(BF16), nt, preferred_element_type=F32)
                 + jnp.dot(a.astype(BF16), vb, preferred_element_type=F32)
                 + jnp.sum(q * k, axis=-1, keepdims=True) * v)
            st = st * e_c[0:1, :] + lax.dot_general(vb, rows(k_c), tn, preferred_element_type=F32)

            ms = jnp.mean(o * o, axis=-1, keepdims=True)
            on = o * lax.rsqrt(ms + EPS) * ng_ref[...]
            o_ref[rs, cs] = (on * _silu(g_ref[rs, cs])).astype(o_ref.dtype)
        st_ref[h] = st


def _hgrn(p, lb_logits, norm_g, *, batch, seq, layer, heads=HGRN_STEP_HEADS,
          chunks=HGRN_STEP_CHUNKS):
    c = chunks * HGRN_CHUNK
    nc = seq // c
    width = heads * HGRN_K
    fdim = HGRN_HEADS * HGRN_K
    nhb = HGRN_HEADS // heads
    masks = jnp.asarray(_hgrn_consts())

    def cols(col0):
        return pl.BlockSpec((c, width), lambda b, h, i: (b * nc + i, col0 // width + h))

    def const(a):
        nd = a.ndim
        return pl.BlockSpec(a.shape, lambda b, h, i: (0,) * nd)

    depth = lb_logits.shape[0]
    ng = norm_g.reshape(1, HGRN_V)
    return pl.pallas_call(
        functools.partial(_hgrn_kernel, heads=heads, chunks=chunks, layer=layer),
        out_shape=jax.ShapeDtypeStruct((batch * seq, D_MODEL), BF16),
        grid=(batch, nhb, nc),
        in_specs=[cols(0), cols(fdim), cols(2 * fdim), cols(2 * fdim + D_MODEL),
                  pl.BlockSpec((depth, width), lambda b, h, i: (0, h)),
                  const(ng), const(masks)],
        out_specs=pl.BlockSpec((c, width), lambda b, h, i: (b * nc + i, h)),
        scratch_shapes=[pltpu.VMEM((heads, HGRN_V, HGRN_K), F32)],
        compiler_params=_params("parallel", "parallel", "arbitrary"),
        name="hgrn2",
    )(p, p, p, p, lb_logits, ng, masks)


def _rope_tables(seq):
    half = QK_ROPE // 2
    inv_freq = (1.0 / (ROPE_BASE ** (np.arange(0, QK_ROPE, 2, dtype=np.float32) / QK_ROPE))
                ).astype(np.float32)
    ang = np.arange(seq, dtype=np.float32)[:, None] * inv_freq[None, :]
    cos, sin = np.cos(ang), np.sin(ang)
    z = np.zeros((seq, half), np.float32)
    zz = np.zeros((seq, LANE - QK_ROPE), np.float32)
    cos_t = np.concatenate([cos, cos, zz], axis=1)
    sin_lo = np.concatenate([-sin, z, zz], axis=1)
    sin_hi = np.concatenate([z, sin, zz], axis=1)
    return jnp.asarray(cos_t), jnp.asarray(sin_lo), jnp.asarray(sin_hi)


def _even_in_weight_kernel(w_ref, o_ref):
    tc = EV_TILE // 3
    for t in range(EV_CONV_TILES):
        for part in range(3):
            src = part * CONV_CH + t * tc
            dst = t * EV_TILE + part * tc
            o_ref[dst:dst + tc, :] = w_ref[src:src + tc, :].astype(BF16)
    mla0 = EV_CONV_TILES * EV_TILE
    cq0 = 3 * CONV_CH
    kpe0 = cq0 + Q_RANK + KV_RANK
    o_ref[mla0:mla0 + MLA_BZ, :] = w_ref[cq0:kpe0, :].astype(BF16)
    o_ref[mla0 + MLA_BZ:mla0 + MLA_KPE, :] = w_ref[kpe0 + QK_ROPE:, :].astype(BF16)
    o_ref[mla0 + MLA_KPE:mla0 + MLA_KPE + QK_ROPE, :] = w_ref[kpe0:kpe0 + QK_ROPE, :].astype(BF16)
    o_ref[mla0 + MLA_KPE + QK_ROPE:, :] = jnp.zeros(
        (EV_MLA_COLS - MLA_KPE - QK_ROPE, o_ref.shape[1]), BF16)


def _even_in_weight(wt, *, tc=WEIGHT_PREP_COLS):
    n, d = wt.shape
    n_out = EV_CONV_TILES * EV_TILE + EV_MLA_COLS
    return pl.pallas_call(
        _even_in_weight_kernel,
        out_shape=jax.ShapeDtypeStruct((n_out, d), BF16),
        grid=(d // tc,),
        in_specs=[pl.BlockSpec((n, tc), lambda i: (0, i))],
        out_specs=pl.BlockSpec((n_out, tc), lambda i: (0, i)),
        compiler_params=_params("parallel"),
        name="even_in_weight",
    )(wt)


def _even_layer(x, h, norm_g, w_in, conv_w, conv_b, ln_g, ln_b, q_norm_g, w_uq, kv_norm_g, w_ukv,
                w_out, next_g, tables, *, batch, seq, last):
    a_out, p = _even_front(x if h is None else h, norm_g, _even_in_weight(w_in.T), conv_w, conv_b,
                           ln_g, ln_b, seq=seq)
    wq = jnp.pad(w_uq, ((0, 0), (0, 0), (0, QK_PAD - QK_NOPE - QK_ROPE)))
    wq = wq.reshape(Q_RANK, MLA_HEADS * QK_PAD).astype(BF16)
    wkv = jnp.concatenate([w_ukv[:, :, :QK_NOPE].reshape(KV_RANK, MLA_WIDTH),
                           w_ukv[:, :, QK_NOPE:].reshape(KV_RANK, MLA_WIDTH)], axis=1).astype(BF16)
    q, kn, v, kp = _mla_proj(p, q_norm_g, kv_norm_g, wq, wkv, *tables, seq=seq)
    b_out = _attention(q, kn, kp, v, p, batch=batch, seq=seq)
    return _out_proj([a_out, b_out], w_out, x, next_g, last=last, name="even_out_proj")


def _odd_layer(x, h, norm_g, w_in, lb_logits, hgrn_norm_g, w_out, next_g, *, batch, seq, layer, last):
    if h is None:
        h = _rms_rows(x, norm_g)
    p = _in_proj(h, w_in, tm=IN_PROJ_ROWS, tn=IN_PROJ_COLS, name="odd_in_proj")
    o = _hgrn(p, lb_logits, hgrn_norm_g, batch=batch, seq=seq, layer=layer)
    return _out_proj([o], w_out, x, next_g, last=last, name="odd_out_proj")


def kernel(x, ev_norm_g, ev_w_in, conv_w, conv_b, conv_ln_g, conv_ln_b, mla_q_norm_g, mla_w_uq,
           mla_kv_norm_g, mla_w_ukv, ev_w_out, od_norm_g, od_w_in, hgrn_lb_logits, hgrn_norm_g,
           od_w_out, final_norm_g):
    batch, seq, d = x.shape
    depth = hgrn_lb_logits.shape[0]
    tables = _rope_tables(seq)
    y = x.reshape(batch * seq, d)
    h = None
    for l in range(depth):
        j = l // 2
        last = l == depth - 1
        if last:
            next_g = final_norm_g
        else:
            next_g = od_norm_g[(l + 1) // 2] if l % 2 == 0 else ev_norm_g[(l + 1) // 2]
        if l % 2 == 0:
            out = _even_layer(y, h, ev_norm_g[j], ev_w_in[j], conv_w[j], conv_b[j], conv_ln_g[j],
                              conv_ln_b[j], mla_q_norm_g[j], mla_w_uq[j], mla_kv_norm_g[j],
                              mla_w_ukv[j], ev_w_out[j], next_g, tables, batch=batch, seq=seq,
                              last=last)
        else:
            out = _odd_layer(y, h, od_norm_g[j], od_w_in[j], hgrn_lb_logits, hgrn_norm_g[j],
                             od_w_out[j], next_g, batch=batch, seq=seq, layer=l, last=last)
        if last:
            y = out
        else:
            y, h = out
    return y.reshape(batch, seq, d)
```

```python
import functools
import math

import numpy as np
import jax
import jax.numpy as jnp
from jax import lax
from jax.experimental import pallas as pl
from jax.experimental.pallas import tpu as pltpu

F32 = jnp.float32
BF16 = jnp.bfloat16

D_MODEL = 2048
CONV_CH = 1024
CONV_GROUP = 128
CONV_K = 31
MLA_HEADS = 8
QK_NOPE = 128
QK_ROPE = 64
V_HEAD = 128
Q_RANK = 512
KV_RANK = 512
ROPE_BASE = 10000.0
MLA_WIDTH = MLA_HEADS * V_HEAD
QK_PAD = 256
HGRN_HEADS = 16
HGRN_K = 128
HGRN_V = 128
EPS = 1e-6

LANE = 128
SUB = 8
HALO = 32
VMEM_LIMIT = 56 * 1024 * 1024

HGRN_CHUNK = 128
HGRN_LEVELS = 7

NORM_ROWS = 512
FRONT_ROWS = 1024
CONV_ROWS = 32
MLA_PROJ_ROWS = 512
ATTN_TILE = 1024
ATTN_HEADS = 2
ATTN_ROW_CHUNK = 256
OUT_PROJ_ROWS = 512
IN_PROJ_ROWS = 1024
IN_PROJ_COLS = 1024
HGRN_STEP_HEADS = 8
HGRN_STEP_CHUNKS = 4
WEIGHT_PREP_COLS = 512


def _params(*sem):
    return pltpu.CompilerParams(dimension_semantics=sem, vmem_limit_bytes=VMEM_LIMIT)


def _sigmoid(x):
    return 1.0 / (1.0 + jnp.exp(-x))


def _silu(x):
    return x * _sigmoid(x)


def _rms(x, g):
    ms = jnp.mean(x * x, axis=-1, keepdims=True)
    return x * lax.rsqrt(ms + EPS) * g


def _rms_rows_kernel(x_ref, g_ref, o_ref):
    o_ref[...] = _rms(x_ref[...], g_ref[...]).astype(o_ref.dtype)


def _rms_rows(x, g, *, tm=NORM_ROWS):
    m, d = x.shape
    return pl.pallas_call(
        _rms_rows_kernel,
        out_shape=jax.ShapeDtypeStruct((m, d), BF16),
        grid=(m // tm,),
        in_specs=[pl.BlockSpec((tm, d), lambda i: (i, 0)), pl.BlockSpec((1, d), lambda i: (0, 0))],
        out_specs=pl.BlockSpec((tm, d), lambda i: (i, 0)),
        compiler_params=_params("parallel"),
        name="rms_rows",
    )(x, g.reshape(1, d))


def _in_proj_kernel(h_ref, w_ref, o_ref, wb_ref):
    @pl.when(pl.program_id(1) == 0)
    def _():
        wb_ref[...] = w_ref[...].astype(BF16)

    o_ref[...] = jnp.dot(h_ref[...], wb_ref[...], preferred_element_type=F32)


def _in_proj(h, w, *, tm, tn, name):
    m, d = h.shape
    n = w.shape[1]
    return pl.pallas_call(
        _in_proj_kernel,
        out_shape=jax.ShapeDtypeStruct((m, n), F32),
        grid=(n // tn, m // tm),
        in_specs=[pl.BlockSpec((tm, d), lambda j, i: (i, 0)),
                  pl.BlockSpec((d, tn), lambda j, i: (0, j))],
        out_specs=pl.BlockSpec((tm, tn), lambda j, i: (i, j)),
        scratch_shapes=[pltpu.VMEM((d, tn), BF16)],
        compiler_params=_params("parallel", "arbitrary"),
        name=name,
    )(h, w)


EV_TILE = 3 * 256
EV_CONV_TILES = CONV_CH // 256
EV_MLA_COLS = 3 * EV_TILE
MLA_CQ, MLA_CKV, MLA_BZ, MLA_KPE = 0, 512, 1024, 2048


def _even_front_kernel(x_ref, g_ref, w_ref, cw_ref, cb_ref, lg_ref, lb_ref, a_ref, p_ref,
                       h_ref, mm_ref, u_ref, halo_ref, *, tm, rows, blocks_per_seq):
    i = pl.program_id(0)
    j = pl.program_id(1)
    tc = EV_TILE // 3

    def project():
        return lax.dot_general(h_ref[...], w_ref[...], (((1,), (1,)), ((), ())),
                               preferred_element_type=F32)

    def conv(slot, c):
        u = mm_ref[slot, :, 0:tc] * _sigmoid(mm_ref[slot, :, tc:2 * tc])
        u_ref[HALO:, :] = u
        u_ref[:HALO, :] = jnp.where(i % blocks_per_seq == 0, 0.0, halo_ref[c])
        halo_ref[c] = u[tm - HALO:, :]
        w = cw_ref[...]
        off = HALO - (CONV_K - 1)
        for r0 in range(0, tm, rows):
            y = cb_ref[...]
            for res in range(SUB):
                taps = [k for k in range(CONV_K) if (off + k) % SUB == res]
                n = rows + (SUB if res else 0)
                z = None
                for k in taps:
                    term = w[k:k + 1, :] * u_ref[pl.ds(r0 + off + k - res, n), :]
                    z = term if z is None else z + term
                y = y + z[res:res + rows, :]
            outs = []
            for c0 in range(0, tc, CONV_GROUP):
                yg = y[:, c0:c0 + CONV_GROUP]
                mu = jnp.mean(yg, axis=-1, keepdims=True)
                d = yg - mu
                var = jnp.mean(d * d, axis=-1, keepdims=True)
                outs.append(d * lax.rsqrt(var + EPS))
            yn = jnp.concatenate(outs, axis=-1) * lg_ref[...] + lb_ref[...]
            z = mm_ref[slot, pl.ds(r0, rows), 2 * tc:3 * tc]
            a_ref[pl.ds(r0, rows), :] = (_silu(yn) * _silu(z)).astype(a_ref.dtype)

    @pl.when((i == 0) & (j == 0))
    def _():
        halo_ref[...] = jnp.zeros(halo_ref.shape, F32)

    @pl.when(j == 0)
    def _():
        if x_ref.dtype == BF16:
            h_ref[...] = x_ref[...]
        else:
            h_ref[...] = _rms(x_ref[...], g_ref[...]).astype(BF16)
        mm_ref[0] = project()

    for t in range(1, EV_CONV_TILES):
        @pl.when(j == t)
        def _(t=t):
            mm_ref[t % 2] = project()
            conv((t - 1) % 2, t - 1)

    @pl.when(j == EV_CONV_TILES)
    def _():
        p_ref[...] = project().astype(p_ref.dtype)
        conv((EV_CONV_TILES - 1) % 2, EV_CONV_TILES - 1)

    @pl.when(j > EV_CONV_TILES)
    def _():
        p_ref[...] = project().astype(p_ref.dtype)


def _even_front(x, g, wt, conv_w, conv_b, ln_g, ln_b, *, seq, tm=FRONT_ROWS, rows=CONV_ROWS):
    m, d = x.shape
    tc = EV_TILE // 3
    n_steps = wt.shape[0] // EV_TILE
    last_conv = EV_CONV_TILES - 1

    def conv_tile(j):
        return jnp.clip(j - 1, 0, last_conv)

    def vec():
        return pl.BlockSpec((1, tc), lambda i, j: (0, conv_tile(j)))

    return pl.pallas_call(
        functools.partial(_even_front_kernel, tm=tm, rows=rows, blocks_per_seq=seq // tm),
        out_shape=(jax.ShapeDtypeStruct((m, CONV_CH), BF16),
                   jax.ShapeDtypeStruct((m, EV_MLA_COLS), BF16)),
        grid=(m // tm, n_steps),
        in_specs=[pl.BlockSpec((tm, d), lambda i, j: (i, 0)),
                  pl.BlockSpec((1, d), lambda i, j: (0, 0)),
                  pl.BlockSpec((EV_TILE, d), lambda i, j: (j, 0)),
                  pl.BlockSpec((CONV_K, tc), lambda i, j: (0, conv_tile(j))),
                  vec(), vec(), vec()],
        out_specs=(pl.BlockSpec((tm, tc), lambda i, j: (i, conv_tile(j))),
                   pl.BlockSpec((tm, EV_TILE), lambda i, j: (i, jnp.maximum(j - EV_CONV_TILES, 0)))),
        scratch_shapes=[pltpu.VMEM((tm, d), BF16),
                        pltpu.VMEM((2, tm, EV_TILE), F32),
                        pltpu.VMEM((HALO + tm, tc), F32),
                        pltpu.VMEM((EV_CONV_TILES, HALO, tc), F32)],
        compiler_params=_params("arbitrary", "arbitrary"),
        name="even_front",
    )(x, g.reshape(1, d), wt, conv_w, conv_b.reshape(1, -1), ln_g.reshape(1, -1), ln_b.reshape(1, -1))


def _rope128(x, cos, sin_lo, sin_hi):
    half = QK_ROPE // 2
    return (x * cos + pltpu.roll(x, LANE - half, axis=1) * sin_lo
            + pltpu.roll(x, half, axis=1) * sin_hi)


def _mla_proj_kernel(cq_ref, ckv_ref, kpe_ref, qg_ref, kvg_ref, wq_ref, wkv_ref,
                     cos_ref, slo_ref, shi_ref, q_ref, kn_ref, v_ref, kp_ref, *, scale):
    cos, slo, shi = cos_ref[...], slo_ref[...], shi_ref[...]
    hq = _rms(cq_ref[...].astype(F32), qg_ref[...]).astype(BF16)
    q = jnp.dot(hq, wq_ref[...], preferred_element_type=F32) * scale
    for h in range(MLA_HEADS):
        c0 = h * QK_PAD
        q_ref[:, c0:c0 + QK_NOPE] = q[:, c0:c0 + QK_NOPE].astype(BF16)
        pe = q[:, c0 + QK_NOPE:c0 + QK_PAD]
        q_ref[:, c0 + QK_NOPE:c0 + QK_PAD] = _rope128(pe, cos, slo, shi).astype(BF16)
    hkv = _rms(ckv_ref[...].astype(F32), kvg_ref[...]).astype(BF16)
    kv = jnp.dot(hkv, wkv_ref[...], preferred_element_type=F32)
    kn_ref[...] = kv[:, :MLA_WIDTH].astype(BF16)
    v_ref[...] = kv[:, MLA_WIDTH:].astype(BF16)
    kp_ref[...] = _rope128(kpe_ref[...].astype(F32), cos, slo, shi).astype(BF16)


def _mla_proj(p, q_norm_g, kv_norm_g, wq, wkv, cos, slo, shi, *, seq, tm=MLA_PROJ_ROWS):
    t = p.shape[0]
    nsb = seq // tm
    scale = float((QK_NOPE + QK_ROPE) ** -0.5 * math.log2(math.e))

    def rows(width, col0):
        return pl.BlockSpec((tm, width), lambda i: (i, col0 // width))

    def full(a):
        return pl.BlockSpec(a.shape, lambda i: (0, 0))

    def table():
        return pl.BlockSpec((tm, LANE), lambda i: (i % nsb, 0))

    qg = q_norm_g.reshape(1, -1)
    kvg = kv_norm_g.reshape(1, -1)
    return pl.pallas_call(
        functools.partial(_mla_proj_kernel, scale=scale),
        out_shape=(jax.ShapeDtypeStruct((t, MLA_HEADS * QK_PAD), BF16),
                   jax.ShapeDtypeStruct((t, MLA_WIDTH), BF16),
                   jax.ShapeDtypeStruct((t, MLA_WIDTH), BF16),
                   jax.ShapeDtypeStruct((t, LANE), BF16)),
        grid=(t // tm,),
        in_specs=[rows(Q_RANK, MLA_CQ), rows(KV_RANK, MLA_CKV), rows(LANE, MLA_KPE),
                  full(qg), full(kvg), full(wq), full(wkv), table(), table(), table()],
        out_specs=(pl.BlockSpec((tm, MLA_HEADS * QK_PAD), lambda i: (i, 0)),
                   pl.BlockSpec((tm, MLA_WIDTH), lambda i: (i, 0)),
                   pl.BlockSpec((tm, MLA_WIDTH), lambda i: (i, 0)),
                   pl.BlockSpec((tm, LANE), lambda i: (i, 0))),
        compiler_params=_params("parallel"),
        name="mla_proj",
    )(p, p, p, qg, kvg, wq, wkv, cos, slo, shi)


def _attn_kernel(q_ref, kn_ref, kp_ref, v_ref, bz_ref, o_ref, kf_ref, vf_ref, m_ref, acc_ref,
                 s_ref, *, tq, heads):
    qi = pl.program_id(2)

    @pl.when(qi == 0)
    def _():
        for h in range(heads):
            kf_ref[h, :, :QK_NOPE] = kn_ref[:, h * QK_NOPE:(h + 1) * QK_NOPE]
            kf_ref[h, :, QK_NOPE:] = kp_ref[...]
            vf_ref[h, :, :V_HEAD] = v_ref[:, h * V_HEAD:(h + 1) * V_HEAD]
            vf_ref[h, :, V_HEAD:] = jnp.ones((vf_ref.shape[1], LANE), BF16)

    m_ref[...] = jnp.full(m_ref.shape, -jnp.inf, F32)
    acc_ref[...] = jnp.zeros(acc_ref.shape, F32)

    def scores(j, slot):
        start = pl.multiple_of(j * tq, tq)
        for h in range(heads):
            q = q_ref[:, h * QK_PAD:(h + 1) * QK_PAD]
            k = kf_ref[h, pl.ds(start, tq), :]
            s_ref[slot, h] = lax.dot_general(q, k, (((1,), (1,)), ((), ())),
                                             preferred_element_type=F32)

    def consume(j, slot, masked):
        start = pl.multiple_of(j * tq, tq)
        for h in range(heads):
            for r0 in range(0, tq, ATTN_ROW_CHUNK):
                rs = slice(r0, r0 + ATTN_ROW_CHUNK)
                s = s_ref[slot, h, rs, :]
                if masked:
                    row = r0 + lax.broadcasted_iota(jnp.int32, s.shape, 0)
                    col = lax.broadcasted_iota(jnp.int32, s.shape, 1)
                    s = jnp.where(col <= row, s, -jnp.inf)
                m_prev = m_ref[h, rs, :]
                m_new = jnp.maximum(m_prev, jnp.max(s, axis=-1, keepdims=True))
                alpha = jnp.exp2(m_prev - m_new)
                p = jnp.exp2(s - jnp.concatenate([m_new] * (tq // LANE), axis=1))
                pv = jnp.dot(p.astype(BF16), vf_ref[h, pl.ds(start, tq), :],
                             preferred_element_type=F32)
                acc_ref[h, rs, :] = acc_ref[h, rs, :] * jnp.concatenate([alpha, alpha], axis=1) + pv
                m_ref[h, rs, :] = m_new

    scores(0, 0)
    pairs = qi // 2

    def body(t, carry):
        scores(2 * t + 1, 1)
        consume(2 * t, 0, False)
        scores(2 * t + 2, 0)
        consume(2 * t + 1, 1, False)
        return carry

    lax.fori_loop(0, pairs, body, 0)

    @pl.when(qi % 2 == 0)
    def _():
        consume(qi, 0, True)

    @pl.when(qi % 2 == 1)
    def _():
        scores(qi, 1)
        consume(qi - 1, 0, False)
        consume(qi, 1, True)

    for h in range(heads):
        cs = slice(h * V_HEAD, (h + 1) * V_HEAD)
        o = acc_ref[h, :, :V_HEAD] / acc_ref[h, :, V_HEAD:]
        o_ref[:, cs] = (o * _silu(bz_ref[:, cs].astype(F32))).astype(o_ref.dtype)


def _attention(q, kn, kp, v, p, *, batch, seq, tq=ATTN_TILE, heads=ATTN_HEADS):
    nq = seq // tq
    vw = heads * V_HEAD
    return pl.pallas_call(
        functools.partial(_attn_kernel, tq=tq, heads=heads),
        out_shape=jax.ShapeDtypeStruct((batch * seq, MLA_WIDTH), BF16),
        grid=(batch, MLA_HEADS // heads, nq),
        in_specs=[
            pl.BlockSpec((tq, heads * QK_PAD), lambda b, h, i: (b * nq + i, h)),
            pl.BlockSpec((seq, heads * QK_NOPE), lambda b, h, i: (b, h)),
            pl.BlockSpec((seq, LANE), lambda b, h, i: (b, 0)),
            pl.BlockSpec((seq, vw), lambda b, h, i: (b, h)),
            pl.BlockSpec((tq, vw), lambda b, h, i: (b * nq + i, MLA_BZ // vw + h)),
        ],
        out_specs=pl.BlockSpec((tq, vw), lambda b, h, i: (b * nq + i, h)),
        scratch_shapes=[pltpu.VMEM((heads, seq, QK_PAD), BF16),
                        pltpu.VMEM((heads, seq, V_HEAD + LANE), BF16),
                        pltpu.VMEM((heads, tq, LANE), F32),
                        pltpu.VMEM((heads, tq, V_HEAD + LANE), F32),
                        pltpu.VMEM((2, heads, tq, tq), F32)],
        compiler_params=_params("parallel", "parallel", "arbitrary"),
        name="mla_attention",
    )(q, kn, kp, v, p)


def _out_proj_kernel(*refs, n_in, emit_y):
    a_refs = refs[:n_in]
    w_ref, x_ref, g_ref = refs[n_in:n_in + 3]
    outs = refs[n_in + 3:-1]
    wb_ref = refs[-1]
    kw = w_ref.shape[0] // n_in

    @pl.when(pl.program_id(0) == 0)
    def _():
        wb_ref[...] = w_ref[...].astype(BF16)

    y = x_ref[...]
    for c, a_ref in enumerate(a_refs):
        y = y + jnp.dot(a_ref[...], wb_ref[c * kw:(c + 1) * kw, :], preferred_element_type=F32)
    if emit_y:
        outs[0][...] = y
    h_ref = outs[-1]
    h_ref[...] = _rms(y, g_ref[...]).astype(h_ref.dtype)


def _out_proj(acts, w, x, norm_g, *, last, tm=OUT_PROJ_ROWS, name):
    m, n = x.shape
    n_in = len(acts)
    kw = w.shape[0] // n_in
    row = pl.BlockSpec((tm, n), lambda i: (i, 0))
    in_specs = [pl.BlockSpec((tm, kw), lambda i: (i, 0)) for _ in acts]
    in_specs += [pl.BlockSpec(w.shape, lambda i: (0, 0), pipeline_mode=pl.Buffered(1)),
                 row, pl.BlockSpec((1, n), lambda i: (0, 0))]
    args = list(acts) + [w, x, norm_g.reshape(1, n)]
    if last:
        out_shape, out_specs = jax.ShapeDtypeStruct((m, n), F32), row
    else:
        out_shape = (jax.ShapeDtypeStruct((m, n), F32), jax.ShapeDtypeStruct((m, n), BF16))
        out_specs = (row, row)
    return pl.pallas_call(
        functools.partial(_out_proj_kernel, n_in=n_in, emit_y=not last),
        out_shape=out_shape,
        grid=(m // tm,),
        in_specs=in_specs,
        out_specs=out_specs,
        scratch_shapes=[pltpu.VMEM(w.shape, BF16)],
        compiler_params=_params("arbitrary"),
        name=name,
    )(*args)


def _hgrn_consts():
    c = HGRN_CHUNK
    t = np.arange(c)[:, None]
    s = np.arange(c)[None, :]
    masks = []
    for lvl in range(HGRN_LEVELS):
        m = 1 << lvl
        same = (t // (2 * m)) == (s // (2 * m))
        masks.append((same & (t % (2 * m) >= m) & (s % (2 * m) < m)).astype(np.float32))
    return np.stack(masks)


def _hgrn_scan(q, k, f):
    nt = HGRN_CHUNK // SUB
    row = lax.broadcasted_iota(jnp.int32, (SUB, HGRN_K), 0)

    def tiles(x):
        return [x[i * SUB:(i + 1) * SUB, :] for i in range(nt)]

    e = tiles(f)
    qs = [a * b for a, b in zip(tiles(q), e)]
    ks = tiles(k)
    levels = []
    m = 1
    while m < SUB:
        levels.append((qs, ks))
        later = (row & m) != 0
        nq, nk, ne = [], [], []
        for i in range(nt):
            down = pltpu.roll(e[i], m, axis=0)
            up = pltpu.roll(e[i], SUB - m, axis=0)
            nq.append(qs[i] * jnp.where(later, down, 1.0))
            nk.append(ks[i] * jnp.where(later, 1.0, up))
            ne.append(e[i] * jnp.where(later, down, up))
        qs, ks, e = nq, nk, ne
        m *= 2
    mt = 1
    while mt < nt:
        levels.append((qs, ks))
        nq, nk, ne = [], [], []
        for i in range(nt):
            if (i // mt) % 2:
                nq.append(qs[i] * e[i - mt])
                nk.append(ks[i])
            else:
                nq.append(qs[i])
                nk.append(ks[i] * e[i + mt])
            ne.append(e[i] * e[i + mt] if i % (2 * mt) == 0 else ne[(i // (2 * mt)) * 2 * mt])
        qs, ks, e = nq, nk, ne
        mt *= 2
    return levels, (qs, ks), e[0]


def _hgrn_kernel(q_ref, f_ref, v_ref, g_ref, lbl_ref, ng_ref, mask_ref, o_ref, st_ref,
                 *, heads, chunks, layer):
    c = HGRN_CHUNK

    @pl.when(pl.program_id(2) == 0)
    def _():
        st_ref[...] = jnp.zeros(st_ref.shape, F32)

    nt = (((1,), (1,)), ((), ()))
    tn = (((0,), (0,)), ((), ()))

    def rows(tiles):
        return jnp.concatenate(tiles, axis=0).astype(BF16)

    def rows_t(tiles):
        return jnp.transpose(jnp.concatenate(tiles, axis=0)).astype(BF16)

    for h in range(heads):
        cs = slice(h * HGRN_K, (h + 1) * HGRN_K)
        logits = lbl_ref[:, cs]
        e = jnp.exp(logits - jnp.max(logits, axis=0, keepdims=True))
        sm = e / jnp.sum(e, axis=0, keepdims=True)
        lb = jnp.sum(sm[:layer + 1], axis=0, keepdims=True) - sm[0:1]

        st = st_ref[h]
        for ci in range(chunks):
            rs = slice(ci * c, (ci + 1) * c)
            q = q_ref[rs, cs]
            v = v_ref[rs, cs]
            gate = (1.0 - lb) * _sigmoid(f_ref[rs, cs])
            f = lb + gate
            k = (1.0 - lb) - gate
            levels, (q_c, k_c), e_c = _hgrn_scan(q, k, f)

            vb = v.astype(BF16)
            a = jnp.zeros((c, c), F32)
            for lvl, (q_l, k_l) in enumerate(levels):
                pair = jnp.dot(rows(q_l), rows_t(k_l), preferred_element_type=F32)
                a = a + mask_ref[lvl] * pair
            o = (lax.dot_general(rows(q_c), st.astype(BF16), nt, preferred_element_type=F32)
                 + jnp.dot(a.astype(BF16), vb, preferred_element_type=F32)
                 + jnp.sum(q * k, axis=-1, keepdims=True) * v)
            st = st * e_c[0:1, :] + lax.dot_general(vb, rows(k_c), tn, preferred_element_type=F32)

            ms = jnp.mean(o * o, axis=-1, keepdims=True)
            on = o * lax.rsqrt(ms + EPS) * ng_ref[...]
            o_ref[rs, cs] = (on * _silu(g_ref[rs, cs])).astype(o_ref.dtype)
        st_ref[h] = st


def _hgrn(p, lb_logits, norm_g, *, batch, seq, layer, heads=HGRN_STEP_HEADS,
          chunks=HGRN_STEP_CHUNKS):
    c = chunks * HGRN_CHUNK
    nc = seq // c
    width = heads * HGRN_K
    fdim = HGRN_HEADS * HGRN_K
    nhb = HGRN_HEADS // heads
    masks = jnp.asarray(_hgrn_consts())

    def cols(col0):
        return pl.BlockSpec((c, width), lambda b, h, i: (b * nc + i, col0 // width + h))

    def const(a):
        nd = a.ndim
        return pl.BlockSpec(a.shape, lambda b, h, i: (0,) * nd)

    depth = lb_logits.shape[0]
    ng = norm_g.reshape(1, HGRN_V)
    return pl.pallas_call(
        functools.partial(_hgrn_kernel, heads=heads, chunks=chunks, layer=layer),
        out_shape=jax.ShapeDtypeStruct((batch * seq, D_MODEL), BF16),
        grid=(batch, nhb, nc),
        in_specs=[cols(0), cols(fdim), cols(2 * fdim), cols(2 * fdim + D_MODEL),
                  pl.BlockSpec((depth, width), lambda b, h, i: (0, h)),
                  const(ng), const(masks)],
        out_specs=pl.BlockSpec((c, width), lambda b, h, i: (b * nc + i, h)),
        scratch_shapes=[pltpu.VMEM((heads, HGRN_V, HGRN_K), F32)],
        compiler_params=_params("parallel", "parallel", "arbitrary"),
        name="hgrn2",
    )(p, p, p, p, lb_logits, ng, masks)


def _rope_tables(seq):
    half = QK_ROPE // 2
    inv_freq = (1.0 / (ROPE_BASE ** (np.arange(0, QK_ROPE, 2, dtype=np.float32) / QK_ROPE))
                ).astype(np.float32)
    ang = np.arange(seq, dtype=np.float32)[:, None] * inv_freq[None, :]
    cos, sin = np.cos(ang), np.sin(ang)
    z = np.zeros((seq, half), np.float32)
    zz = np.zeros((seq, LANE - QK_ROPE), np.float32)
    cos_t = np.concatenate([cos, cos, zz], axis=1)
    sin_lo = np.concatenate([-sin, z, zz], axis=1)
    sin_hi = np.concatenate([z, sin, zz], axis=1)
    return jnp.asarray(cos_t), jnp.asarray(sin_lo), jnp.asarray(sin_hi)


def _even_in_weight_kernel(w_ref, o_ref):
    tc = EV_TILE // 3
    for t in range(EV_CONV_TILES):
        for part in range(3):
            src = part * CONV_CH + t * tc
            dst = t * EV_TILE + part * tc
            o_ref[dst:dst + tc, :] = w_ref[src:src + tc, :].astype(BF16)
    mla0 = EV_CONV_TILES * EV_TILE
    cq0 = 3 * CONV_CH
    kpe0 = cq0 + Q_RANK + KV_RANK
    o_ref[mla0:mla0 + MLA_BZ, :] = w_ref[cq0:kpe0, :].astype(BF16)
    o_ref[mla0 + MLA_BZ:mla0 + MLA_KPE, :] = w_ref[kpe0 + QK_ROPE:, :].astype(BF16)
    o_ref[mla0 + MLA_KPE:mla0 + MLA_KPE + QK_ROPE, :] = w_ref[kpe0:kpe0 + QK_ROPE, :].astype(BF16)
    o_ref[mla0 + MLA_KPE + QK_ROPE:, :] = jnp.zeros(
        (EV_MLA_COLS - MLA_KPE - QK_ROPE, o_ref.shape[1]), BF16)


def _even_in_weight(wt, *, tc=WEIGHT_PREP_COLS):
    n, d = wt.shape
    n_out = EV_CONV_TILES * EV_TILE + EV_MLA_COLS
    return pl.pallas_call(
        _even_in_weight_kernel,
        out_shape=jax.ShapeDtypeStruct((n_out, d), BF16),
        grid=(d // tc,),
        in_specs=[pl.BlockSpec((n, tc), lambda i: (0, i))],
        out_specs=pl.BlockSpec((n_out, tc), lambda i: (0, i)),
        compiler_params=_params("parallel"),
        name="even_in_weight",
    )(wt)


def _even_layer(x, h, norm_g, w_in, conv_w, conv_b, ln_g, ln_b, q_norm_g, w_uq, kv_norm_g, w_ukv,
                w_out, next_g, tables, *, batch, seq, last):
    a_out, p = _even_front(x if h is None else h, norm_g, _even_in_weight(w_in.T), conv_w, conv_b,
                           ln_g, ln_b, seq=seq)
    wq = jnp.pad(w_uq, ((0, 0), (0, 0), (0, QK_PAD - QK_NOPE - QK_ROPE)))
    wq = wq.reshape(Q_RANK, MLA_HEADS * QK_PAD).astype(BF16)
    wkv = jnp.concatenate([w_ukv[:, :, :QK_NOPE].reshape(KV_RANK, MLA_WIDTH),
                           w_ukv[:, :, QK_NOPE:].reshape(KV_RANK, MLA_WIDTH)], axis=1).astype(BF16)
    q, kn, v, kp = _mla_proj(p, q_norm_g, kv_norm_g, wq, wkv, *tables, seq=seq)
    b_out = _attention(q, kn, kp, v, p, batch=batch, seq=seq)
    return _out_proj([a_out, b_out], w_out, x, next_g, last=last, name="even_out_proj")


def _odd_layer(x, h, norm_g, w_in, lb_logits, hgrn_norm_g, w_out, next_g, *, batch, seq, layer, last):
    if h is None:
        h = _rms_rows(x, norm_g)
    p = _in_proj(h, w_in, tm=IN_PROJ_ROWS, tn=IN_PROJ_COLS, name="odd_in_proj")
    o = _hgrn(p, lb_logits, hgrn_norm_g, batch=batch, seq=seq, layer=layer)
    return _out_proj([o], w_out, x, next_g, last=last, name="odd_out_proj")


def kernel(x, ev_norm_g, ev_w_in, conv_w, conv_b, conv_ln_g, conv_ln_b, mla_q_norm_g, mla_w_uq,
           mla_kv_norm_g, mla_w_ukv, ev_w_out, od_norm_g, od_w_in, hgrn_lb_logits, hgrn_norm_g,
           od_w_out, final_norm_g):
    batch, seq, d = x.shape
    depth = hgrn_lb_logits.shape[0]
    tables = _rope_tables(seq)
    y = x.reshape(batch * seq, d)
    h = None
    for l in range(depth):
        j = l // 2
        last = l == depth - 1
        if last:
            next_g = final_norm_g
        else:
            next_g = od_norm_g[(l + 1) // 2] if l % 2 == 0 else ev_norm_g[(l + 1) // 2]
        if l % 2 == 0:
            out = _even_layer(y, h, ev_norm_g[j], ev_w_in[j], conv_w[j], conv_b[j], conv_ln_g[j],
                              conv_ln_b[j], mla_q_norm_g[j], mla_w_uq[j], mla_kv_norm_g[j],
                              mla_w_ukv[j], ev_w_out[j], next_g, tables, batch=batch, seq=seq,
                              last=last)
        else:
            out = _odd_layer(y, h, od_norm_g[j], od_w_in[j], hgrn_lb_logits, hgrn_norm_g[j],
                             od_w_out[j], next_g, batch=batch, seq=seq, layer=l, last=last)
        if last:
            y = out
        else:
            y, h = out
    return y.reshape(batch, seq, d)
```
